```python
import math
import jax
import jax.numpy as jnp
from jax import lax
import numpy as np

D_MODEL = 4096
BATCH = 8
SEQ = 2048
DEPTH = 4

CHUNK = 64
Q_BLOCK = 128
N_MIXERS = 4
FFN_DIM = 5632
D_COND = 256
EPS = 1e-6

SC_WIDTH = 3
MLA_HEADS = 32
MLA_Q_RANK = 1024
MLA_KV_RANK = 512
MLA_NOPE = 128
MLA_ROPE = 64
MLA_V = 128
MLA_QK = MLA_NOPE + MLA_ROPE
ROPE_BASE = 10000.0
LRU_WIDTH = D_MODEL
LRU_BLOCKS = 16
LRU_BLOCK = LRU_WIDTH // LRU_BLOCKS
LRU_CONV = 4
LRU_C = 8.0
GDN_K_HEADS = 16
GDN_V_HEADS = 32
GDN_HEAD = 128
GDN_CONV = 4
GDN_KD = GDN_K_HEADS * GDN_HEAD
GDN_VD = GDN_V_HEADS * GDN_HEAD
GDN_QKV = 2 * GDN_KD + GDN_VD
GDN_IN = GDN_QKV + GDN_VD + 2 * GDN_V_HEADS

kernel_name = 'hybrid_streaming_encoder_trunk'


def _n_uses(m):
    return len(range(m, DEPTH, N_MIXERS))


def rmsnorm(x, g):
    xf = x.astype(jnp.float32)
    y = xf * lax.rsqrt(jnp.mean(xf * xf, axis=-1, keepdims=True) + EPS)
    return (y * g.astype(jnp.float32)).astype(x.dtype)


def l2norm(t):
    return t * lax.rsqrt(jnp.sum(t * t, axis=-1, keepdims=True) + EPS)


def adaln(x, g, shift, scale):
    return rmsnorm(x, g) * (1 + scale[:, None, :]) + shift[:, None, :]


def causal_dwconv(x, w):
    k, ch = w.shape
    return lax.conv_general_dilated(x, w.astype(x.dtype)[:, None, :], window_strides=(1,),
                                    padding=[(k - 1, 0)], dimension_numbers=('NWC', 'WIO', 'NWC'),
                                    feature_group_count=ch)


def swiglu(h, w_in, w_out):
    gt, up = jnp.split(h @ w_in, 2, axis=-1)
    return (jax.nn.silu(gt) * up) @ w_out


def rope_tables(positions):
    inv = ROPE_BASE ** (-jnp.arange(0, MLA_ROPE, 2, dtype=jnp.float32) / MLA_ROPE)
    ang = positions.astype(jnp.float32)[..., None] * inv
    return jnp.cos(ang)[:, :, None, :], jnp.sin(ang)[:, :, None, :]


def apply_rope(t, cos, sin):
    t1, t2 = jnp.split(t.astype(jnp.float32), 2, axis=-1)
    return jnp.concatenate([t1 * cos - t2 * sin, t2 * cos + t1 * sin], axis=-1).astype(t.dtype)


def short_conv_mixer(h, w_in, conv_w, w_out):
    b_gate, c_gate, xin = jnp.split(h @ w_in, 3, axis=-1)
    return (b_gate * causal_dwconv(c_gate * xin, conv_w)) @ w_out


def mla_mixer(h, cos, sin, w_in, q_lat_g, kv_lat_g, w_uq, w_ukv, q_norm_g, k_norm_g, w_o):
    B, S, _ = h.shape
    lat = h @ w_in
    q_lat = lat[..., :MLA_Q_RANK]
    kv_lat = lat[..., MLA_Q_RANK:MLA_Q_RANK + MLA_KV_RANK]
    k_pe = lat[..., MLA_Q_RANK + MLA_KV_RANK:]
    q = (rmsnorm(q_lat, q_lat_g) @ w_uq).reshape(B, S, MLA_HEADS, MLA_QK)
    kv = (rmsnorm(kv_lat, kv_lat_g) @ w_ukv).reshape(B, S, MLA_HEADS, MLA_NOPE + MLA_V)
    k_nope, v = kv[..., :MLA_NOPE], kv[..., MLA_NOPE:]
    k = jnp.concatenate([k_nope, jnp.broadcast_to(k_pe[:, :, None, :], (B, S, MLA_HEADS, MLA_ROPE))], axis=-1)
    q = rmsnorm(q, q_norm_g)
    k = rmsnorm(k, k_norm_g)
    q = jnp.concatenate([q[..., :MLA_NOPE], apply_rope(q[..., MLA_NOPE:], cos, sin)], axis=-1)
    k = jnp.concatenate([k[..., :MLA_NOPE], apply_rope(k[..., MLA_NOPE:], cos, sin)], axis=-1)
    n_blk = S // Q_BLOCK
    q_blocks = q.reshape(B, n_blk, Q_BLOCK, MLA_HEADS, MLA_QK).swapaxes(0, 1)
    key_chunk = jnp.arange(S) // CHUNK
    scale = MLA_QK ** -0.5

    def attend(args):
        qb, blk = args
        q_chunk = (blk * Q_BLOCK + jnp.arange(Q_BLOCK)) // CHUNK
        mask = key_chunk[None, :] <= q_chunk[:, None]
        s = jnp.einsum('bqhd,bkhd->bhqk', qb, k, preferred_element_type=jnp.float32) * scale
        p = jax.nn.softmax(jnp.where(mask, s, -1e30), axis=-1)
        return jnp.einsum('bhqk,bkhd->bqhd', p.astype(v.dtype), v)

    o = lax.map(attend, (q_blocks, jnp.arange(n_blk)))
    o = o.swapaxes(0, 1).reshape(B, S, MLA_HEADS * MLA_V)
    return o @ w_o


def rglru_mixer(h, w_in, conv_w, conv_b, w_a, b_a, w_x, b_x, lam, w_out):
    B, S, _ = h.shape
    gate, xr = jnp.split(h @ w_in, 2, axis=-1)
    xr = causal_dwconv(xr, conv_w) + conv_b
    xb = xr.reshape(B, S, LRU_BLOCKS, LRU_BLOCK)
    r = jax.nn.sigmoid((jnp.einsum('bsnd,nde->bsne', xb, w_a).reshape(B, S, LRU_WIDTH) + b_a).astype(jnp.float32))
    i = jax.nn.sigmoid((jnp.einsum('bsnd,nde->bsne', xb, w_x).reshape(B, S, LRU_WIDTH) + b_x).astype(jnp.float32))
    log_a = -LRU_C * r * jax.nn.softplus(-lam.astype(jnp.float32))
    a = jnp.exp(log_a)
    u = jnp.sqrt(-jnp.expm1(2.0 * log_a)) * (i * xr.astype(jnp.float32))

    def combine(left, right):
        a_l, b_l = left
        a_r, b_r = right
        return a_l * a_r, a_r * b_l + b_r

    _, hs = lax.associative_scan(combine, (a, u), axis=1)
    return (hs.astype(h.dtype) * jax.nn.gelu(gate)) @ w_out


def chunk_gated_delta_rule(q, k, v, g, beta):
    B, S, H, dk = q.shape
    dv = v.shape[-1]
    n = S // CHUNK

    def to_chunks(t):
        return t.reshape(B, n, CHUNK, H, -1).transpose(1, 0, 3, 2, 4)

    q, k, v = to_chunks(q), to_chunks(k), to_chunks(v)
    g = jnp.cumsum(g.reshape(B, n, CHUNK, H).transpose(1, 0, 3, 2), axis=-1)
    beta = beta.reshape(B, n, CHUNK, H).transpose(1, 0, 3, 2)
    tri = jnp.tril(jnp.ones((CHUNK, CHUNK), dtype=bool))
    strict = jnp.tril(jnp.ones((CHUNK, CHUNK), dtype=bool), k=-1)
    diff = g[..., :, None] - g[..., None, :]
    decay = jnp.where(tri, jnp.exp(jnp.where(tri, diff, 0.0)), 0.0)
    kb = k * beta[..., None]
    m = jnp.where(strict, jnp.einsum('nbhik,nbhjk->nbhij', kb, k) * decay, 0.0)
    eye = jnp.eye(CHUNK, dtype=jnp.float32)
    t_inv = lax.linalg.triangular_solve(m + eye, jnp.broadcast_to(eye, m.shape), left_side=True,
                                        lower=True, unit_diagonal=True)
    u = jnp.einsum('nbhij,nbhjv->nbhiv', t_inv, v * beta[..., None])
    w = jnp.einsum('nbhij,nbhjk->nbhik', t_inv, kb * jnp.exp(g)[..., None])
    a_intra = jnp.einsum('nbhik,nbhjk->nbhij', q, k) * decay

    def step(state, xs):
        q_i, k_i, u_i, w_i, g_i, a_i = xs
        v_new = u_i - jnp.einsum('bhck,bhkv->bhcv', w_i, state)
        o = (jnp.einsum('bhck,bhkv->bhcv', q_i * jnp.exp(g_i)[..., None], state)
             + jnp.einsum('bhcj,bhjv->bhcv', a_i, v_new))
        g_last = g_i[..., -1]
        state = (state * jnp.exp(g_last)[..., None, None]
                 + jnp.einsum('bhck,bhcv->bhkv', k_i * jnp.exp(g_last[..., None] - g_i)[..., None], v_new))
        return state, o

    state0 = jnp.zeros((B, H, dk, dv), jnp.float32)
    _, o = lax.scan(step, state0, (q, k, u, w, g, a_intra))
    return o.transpose(1, 0, 3, 2, 4).reshape(B, S, H, dv)


def gdn_mixer(h, w_in, conv_w, a_log, dt_bias, o_norm_g, w_out):
    B, S, _ = h.shape
    f32 = jnp.float32
    proj = h @ w_in
    qkv = jax.nn.silu(causal_dwconv(proj[..., :GDN_QKV], conv_w))
    z = proj[..., GDN_QKV:GDN_QKV + GDN_VD].reshape(B, S, GDN_V_HEADS, GDN_HEAD)
    b_raw = proj[..., GDN_QKV + GDN_VD:GDN_QKV + GDN_VD + GDN_V_HEADS]
    a_raw = proj[..., GDN_QKV + GDN_VD + GDN_V_HEADS:]
    q = qkv[..., :GDN_KD].reshape(B, S, GDN_K_HEADS, GDN_HEAD).astype(f32)
    k = qkv[..., GDN_KD:2 * GDN_KD].reshape(B, S, GDN_K_HEADS, GDN_HEAD).astype(f32)
    v = qkv[..., 2 * GDN_KD:].reshape(B, S, GDN_V_HEADS, GDN_HEAD).astype(f32)
    rep = GDN_V_HEADS // GDN_K_HEADS
    q = jnp.repeat(l2norm(q), rep, axis=2) * (GDN_HEAD ** -0.5)
    k = jnp.repeat(l2norm(k), rep, axis=2)
    beta = jax.nn.sigmoid(b_raw.astype(f32))
    g = -jnp.exp(a_log.astype(f32)) * jax.nn.softplus(a_raw.astype(f32) + dt_bias.astype(f32))
    o = chunk_gated_delta_rule(q, k, v, g, beta)
    o = rmsnorm(o, o_norm_g) * jax.nn.silu(z.astype(f32))
    return o.reshape(B, S, GDN_VD).astype(h.dtype) @ w_out


def setup_inputs(seed: int = 0):
    key = jax.random.key(seed)
    keys = iter(jax.random.split(key, 64))
    f32 = jnp.float32

    def nrm(shape, fan_in, mult=1.0):
        return jax.random.normal(next(keys), shape, f32) * (mult * fan_in ** -0.5)

    def gain(shape):
        return 1.0 + 0.1 * jax.random.normal(next(keys), shape, f32)

    def bias(shape):
        return 0.01 * jax.random.normal(next(keys), shape, f32)

    n_a, n_b, n_c, n_d = (_n_uses(m) for m in range(N_MIXERS))
    x = jax.random.normal(next(keys), (BATCH, SEQ, D_MODEL), f32)
    c = jax.random.normal(next(keys), (BATCH, D_MODEL), f32)
    offsets = jax.random.randint(next(keys), (BATCH, 1), 0, 8192, dtype=jnp.int32)
    positions = offsets + jnp.arange(SEQ, dtype=jnp.int32)[None, :]
    a0 = jax.random.uniform(next(keys), (n_c, LRU_WIDTH), f32, 0.9, 0.999)
    a_init = jax.random.uniform(next(keys), (n_d, GDN_V_HEADS), f32, 1.0, 16.0)
    dt = jnp.exp(jax.random.uniform(next(keys), (n_d, GDN_V_HEADS), f32, math.log(1e-3), math.log(1e-1)))
    return {
        'x': x,
        'c': c,
        'positions': positions,
        'cond_w': nrm((D_MODEL, D_COND), D_MODEL),
        'cond_b': bias((D_COND,)),
        'mod_w': nrm((DEPTH, D_COND, 9 * D_MODEL), D_COND, 0.5),
        'mod_b': bias((DEPTH, 9 * D_MODEL)),
        'norm_g': gain((DEPTH, 3, D_MODEL)),
        'ffn_w_in': nrm((DEPTH, 2, D_MODEL, 2 * FFN_DIM), D_MODEL),
        'ffn_w_out': nrm((DEPTH, 2, FFN_DIM, D_MODEL), FFN_DIM),
        'sc_w_in': nrm((n_a, D_MODEL, 3 * D_MODEL), D_MODEL),
        'sc_conv_w': nrm((n_a, SC_WIDTH, D_MODEL), SC_WIDTH),
        'sc_w_out': nrm((n_a, D_MODEL, D_MODEL), D_MODEL),
        'mla_w_in': nrm((n_b, D_MODEL, MLA_Q_RANK + MLA_KV_RANK + MLA_ROPE), D_MODEL),
        'mla_q_lat_g': gain((n_b, MLA_Q_RANK)),
        'mla_kv_lat_g': gain((n_b, MLA_KV_RANK)),
        'mla_w_uq': nrm((n_b, MLA_Q_RANK, MLA_HEADS * MLA_QK), MLA_Q_RANK),
        'mla_w_ukv': nrm((n_b, MLA_KV_RANK, MLA_HEADS * (MLA_NOPE + MLA_V)), MLA_KV_RANK),
        'mla_q_norm_g': gain((n_b, MLA_QK)),
        'mla_k_norm_g': gain((n_b, MLA_QK)),
        'mla_w_o': nrm((n_b, MLA_HEADS * MLA_V, D_MODEL), MLA_HEADS * MLA_V),
        'lru_w_in': nrm((n_c, D_MODEL, 2 * LRU_WIDTH), D_MODEL),
        'lru_conv_w': nrm((n_c, LRU_CONV, LRU_WIDTH), LRU_CONV),
        'lru_conv_b': bias((n_c, LRU_WIDTH)),
        'lru_w_a': nrm((n_c, LRU_BLOCKS, LRU_BLOCK, LRU_BLOCK), LRU_BLOCK),
        'lru_b_a': bias((n_c, LRU_WIDTH)),
        'lru_w_x': nrm((n_c, LRU_BLOCKS, LRU_BLOCK, LRU_BLOCK), LRU_BLOCK),
        'lru_b_x': bias((n_c, LRU_WIDTH)),
        'lru_lam': jnp.log(a0) - jnp.log1p(-a0),
        'lru_w_out': nrm((n_c, LRU_WIDTH, D_MODEL), LRU_WIDTH),
        'gdn_w_in': nrm((n_d, D_MODEL, GDN_IN), D_MODEL),
        'gdn_conv_w': nrm((n_d, GDN_CONV, GDN_QKV), GDN_CONV),
        'gdn_a_log': jnp.log(a_init),
        'gdn_dt_bias': dt + jnp.log(-jnp.expm1(-dt)),
        'gdn_o_norm_g': gain((n_d, GDN_HEAD)),
        'gdn_w_out': nrm((n_d, GDN_VD, D_MODEL), GDN_VD),
    }


def reference(x, c, positions, cond_w, cond_b, mod_w, mod_b, norm_g, ffn_w_in, ffn_w_out,
              sc_w_in, sc_conv_w, sc_w_out,
              mla_w_in, mla_q_lat_g, mla_kv_lat_g, mla_w_uq, mla_w_ukv, mla_q_norm_g, mla_k_norm_g, mla_w_o,
              lru_w_in, lru_conv_w, lru_conv_b, lru_w_a, lru_b_a, lru_w_x, lru_b_x, lru_lam, lru_w_out,
              gdn_w_in, gdn_conv_w, gdn_a_log, gdn_dt_bias, gdn_o_norm_g, gdn_w_out):
    B, S, D = x.shape
    cos, sin = rope_tables(positions)
    cond = jax.nn.silu(c @ cond_w + cond_b)
    for i in range(DEPTH):
        m, j = i % N_MIXERS, i // N_MIXERS
        mod = (cond @ mod_w[i] + mod_b[i]).reshape(B, 3, 3, D)
        shift, scale, gate = mod[:, :, 0], mod[:, :, 1], mod[:, :, 2]
        h = adaln(x, norm_g[i, 0], shift[:, 0], scale[:, 0])
        x = x + 0.5 * (1 + gate[:, 0, None, :]) * swiglu(h, ffn_w_in[i, 0], ffn_w_out[i, 0])
        h = adaln(x, norm_g[i, 1], shift[:, 1], scale[:, 1])
        if m == 0:
            mix = short_conv_mixer(h, sc_w_in[j], sc_conv_w[j], sc_w_out[j])
        elif m == 1:
            mix = mla_mixer(h, cos, sin, mla_w_in[j], mla_q_lat_g[j], mla_kv_lat_g[j], mla_w_uq[j],
                            mla_w_ukv[j], mla_q_norm_g[j], mla_k_norm_g[j], mla_w_o[j])
        elif m == 2:
            mix = rglru_mixer(h, lru_w_in[j], lru_conv_w[j], lru_conv_b[j], lru_w_a[j], lru_b_a[j],
                              lru_w_x[j], lru_b_x[j], lru_lam[j], lru_w_out[j])
        else:
            mix = gdn_mixer(h, gdn_w_in[j], gdn_conv_w[j], gdn_a_log[j], gdn_dt_bias[j],
                            gdn_o_norm_g[j], gdn_w_out[j])
        x = x + (1 + gate[:, 1, None, :]) * mix
        h = adaln(x, norm_g[i, 2], shift[:, 2], scale[:, 2])
        x = x + 0.5 * (1 + gate[:, 2, None, :]) * swiglu(h, ffn_w_in[i, 1], ffn_w_out[i, 1])
    return x
```

```python
import functools

import jax
import jax.numpy as jnp
import numpy as np
from jax import lax
from jax.experimental import pallas as pl
from jax.experimental.pallas import tpu as pltpu

F32 = jnp.float32
BF16 = jnp.bfloat16

EPS = 1e-6
CHUNK = 64
MLA_HEADS = 32
MLA_Q_RANK = 1024
MLA_KV_RANK = 512
MLA_NOPE = 128
MLA_ROPE = 64
MLA_V = 128
MLA_QK = MLA_NOPE + MLA_ROPE
MLA_SLOT = 256
ROPE_BASE = 10000.0
LRU_BLOCK = 256
LRU_C = 8.0
GDN_K_HEADS = 16
GDN_V_HEADS = 32
GDN_HEAD = 128
GDN_KD = GDN_K_HEADS * GDN_HEAD
GDN_VD = GDN_V_HEADS * GDN_HEAD
GDN_QKV = 2 * GDN_KD + GDN_VD

LANES = 128
SUBLANES = 8
VMEM_LIMIT = 56 * 1024 * 1024


def _params(sem):
    return pltpu.CompilerParams(dimension_semantics=sem, vmem_limit_bytes=VMEM_LIMIT)


def _mm(a, b):
    return jnp.dot(a, b, preferred_element_type=F32)


def _mm_nt(a, b):
    return lax.dot_general(a, b, (((1,), (1,)), ((), ())), preferred_element_type=F32)


def _mm_tn(a, b):
    return lax.dot_general(a, b, (((0,), (0,)), ((), ())), preferred_element_type=F32)


def _sigmoid(x):
    return 1.0 / (1.0 + jnp.exp(-x))


def _silu(x):
    return x * _sigmoid(x)


def _softplus(x):
    return jnp.maximum(x, 0.0) + jnp.log(1.0 + jnp.exp(-jnp.abs(x)))


def _gelu_tanh(x):
    c = np.sqrt(2.0 / np.pi).astype(np.float32)
    return 0.5 * x * (1.0 + jnp.tanh(c * (x + 0.044715 * (x * x * x))))


def _mod_kernel(c_ref, cw_ref, cb_ref, mw_ref, mb_ref, o_ref):
    cond = _silu(_mm(c_ref[...].astype(BF16), cw_ref[...].astype(BF16)) + cb_ref[...])
    o_ref[...] = _mm(cond.astype(BF16), mw_ref[...].astype(BF16)) + mb_ref[...]


def _modulation(c, cond_w, cond_b, mod_w, mod_b):
    depth, d_cond, n = mod_w.shape
    b, d = c.shape
    tn = d
    return pl.pallas_call(
        _mod_kernel,
        grid=(depth, n // tn),
        in_specs=[
            pl.BlockSpec((b, d), lambda i, j: (0, 0)),
            pl.BlockSpec((d, d_cond), lambda i, j: (0, 0)),
            pl.BlockSpec((1, d_cond), lambda i, j: (0, 0)),
            pl.BlockSpec((None, d_cond, tn), lambda i, j: (i, 0, j)),
            pl.BlockSpec((None, 1, tn), lambda i, j: (i, 0, j)),
        ],
        out_specs=pl.BlockSpec((None, b, tn), lambda i, j: (i, 0, j)),
        out_shape=jax.ShapeDtypeStruct((depth, b, n), F32),
        compiler_params=_params(("arbitrary", "arbitrary")),
        name="modulation",
    )(c, cond_w, cond_b.reshape(1, d_cond), mod_w, mod_b.reshape(depth, 1, n))


def _prenorm_kernel(x_ref, g_ref, sc_ref, sh_ref, o_ref):
    x = x_ref[...]
    y = x * lax.rsqrt(jnp.mean(x * x, axis=-1, keepdims=True) + EPS)
    o_ref[...] = ((y * g_ref[...]) * (1.0 + sc_ref[...]) + sh_ref[...]).astype(o_ref.dtype)


def _prenorm(x2, g, scale, shift, seq):
    t, d = x2.shape
    tm = 256
    tpb = seq // tm
    return pl.pallas_call(
        _prenorm_kernel,
        grid=(t // tm,),
        in_specs=[
            pl.BlockSpec((tm, d), lambda i: (i, 0)),
            pl.BlockSpec((1, d), lambda i: (0, 0)),
            pl.BlockSpec((None, 1, d), lambda i: (i // tpb, 0, 0)),
            pl.BlockSpec((None, 1, d), lambda i: (i // tpb, 0, 0)),
        ],
        out_specs=pl.BlockSpec((tm, d), lambda i: (i, 0)),
        out_shape=jax.ShapeDtypeStruct((t, d), BF16),
        compiler_params=_params(("arbitrary",)),
        name="prenorm",
    )(x2, g.reshape(1, d), scale[:, None, :], shift[:, None, :])


def _swiglu_kernel(h_ref, wg_ref, wu_ref, o_ref):
    a = h_ref[...]
    gt = _mm(a, wg_ref[...])
    up = _mm(a, wu_ref[...])
    o_ref[...] = (_silu(gt) * up).astype(o_ref.dtype)


def _swiglu_in(hn, w_in):
    t, d = hn.shape
    f = w_in.shape[1] // 2
    tm, tn = 1024, 512
    nj = f // tn
    return pl.pallas_call(
        _swiglu_kernel,
        grid=(t // tm, nj),
        in_specs=[
            pl.BlockSpec((tm, d), lambda i, j: (i, 0)),
            pl.BlockSpec((d, tn), lambda i, j: (0, j)),
            pl.BlockSpec((d, tn), lambda i, j: (0, j + nj)),
        ],
        out_specs=pl.BlockSpec((tm, tn), lambda i, j: (i, j)),
        out_shape=jax.ShapeDtypeStruct((t, f), BF16),
        compiler_params=_params(("arbitrary", "arbitrary")),
        name="swiglu_in",
    )(hn, w_in, w_in)


def _mm_res_kernel(a_ref, w_ref, r_ref, al_ref, o_ref):
    o_ref[...] = r_ref[...] + al_ref[...] * _mm(a_ref[...], w_ref[...])


def _matmul_residual(a, w, res, alpha, seq):
    t, k = a.shape
    n = w.shape[1]
    tm, tn = 1024, 512
    tpb = seq // tm
    return pl.pallas_call(
        _mm_res_kernel,
        grid=(t // tm, n // tn),
        in_specs=[
            pl.BlockSpec((tm, k), lambda i, j: (i, 0)),
            pl.BlockSpec((k, tn), lambda i, j: (0, j)),
            pl.BlockSpec((tm, tn), lambda i, j: (i, j)),
            pl.BlockSpec((None, 1, tn), lambda i, j: (i // tpb, 0, j)),
        ],
        out_specs=pl.BlockSpec((tm, tn), lambda i, j: (i, j)),
        out_shape=jax.ShapeDtypeStruct((t, n), F32),
        compiler_params=_params(("arbitrary", "arbitrary")),
        name="matmul_residual",
    )(a, w, res, alpha[:, None, :])


def _mm_kernel(a_ref, w_ref, o_ref):
    o_ref[...] = _mm(a_ref[...], w_ref[...]).astype(o_ref.dtype)


def _matmul(a, w, n, col_off, tn, out_dtype):
    t, k = a.shape
    tm = 1024
    joff = col_off // tn
    return pl.pallas_call(
        _mm_kernel,
        grid=(t // tm, n // tn),
        in_specs=[
            pl.BlockSpec((tm, k), lambda i, j: (i, 0)),
            pl.BlockSpec((k, tn), lambda i, j: (0, j + joff)),
        ],
        out_specs=pl.BlockSpec((tm, tn), lambda i, j: (i, j)),
        out_shape=jax.ShapeDtypeStruct((t, n), out_dtype),
        compiler_params=_params(("arbitrary", "arbitrary")),
        name="matmul",
    )(a, w)


def _shift_rows(z, halo, k, row8):
    zk = pltpu.roll(z, k, 0)
    hk = pltpu.roll(halo, k, 0)
    top = jnp.where(row8 < k, hk, zk[:SUBLANES])
    return jnp.concatenate([top, zk[SUBLANES:]], axis=0)


def _causal_conv(z, cw_ref, halo, width):
    row8 = lax.broadcasted_iota(jnp.int32, (SUBLANES, z.shape[1]), 0)
    acc = z * cw_ref[width - 1:width, :]
    for k in range(1, width):
        acc = acc + _shift_rows(z, halo, k, row8) * cw_ref[width - 1 - k:width - k, :]
    return acc


def _sconv_kernel(h_ref, wb_ref, wc_ref, wx_ref, cw_ref, o_ref, halo_ref, *, tm, width):
    @pl.when(pl.program_id(2) == 0)
    def _():
        halo_ref[...] = jnp.zeros_like(halo_ref)

    a = h_ref[...]
    bg = _mm(a, wb_ref[...])
    z = _mm(a, wc_ref[...]) * _mm(a, wx_ref[...])
    conv = _causal_conv(z, cw_ref, halo_ref[...], width)
    halo_ref[...] = z[tm - SUBLANES:, :]
    o_ref[...] = (bg * conv).astype(o_ref.dtype)


def _sconv_in(hn, w_in, conv_w, batch, seq):
    t, d = hn.shape
    width = conv_w.shape[0]
    tm, tn = 1024, 256
    ns, nj = seq // tm, d // tn
    return pl.pallas_call(
        functools.partial(_sconv_kernel, tm=tm, width=width),
        grid=(nj, batch, ns),
        in_specs=[
            pl.BlockSpec((tm, d), lambda j, b, s: (b * ns + s, 0)),
            pl.BlockSpec((d, tn), lambda j, b, s: (0, j)),
            pl.BlockSpec((d, tn), lambda j, b, s: (0, j + nj)),
            pl.BlockSpec((d, tn), lambda j, b, s: (0, j + 2 * nj)),
            pl.BlockSpec((width, tn), lambda j, b, s: (0, j)),
        ],
        out_specs=pl.BlockSpec((tm, tn), lambda j, b, s: (b * ns + s, j)),
        out_shape=jax.ShapeDtypeStruct((t, d), BF16),
        scratch_shapes=[pltpu.VMEM((SUBLANES, tn), F32)],
        compiler_params=_params(("arbitrary", "arbitrary", "arbitrary")),
        name="sconv_in",
    )(hn, w_in, w_in, w_in, conv_w)


def _lru_kernel(h_ref, wg_ref, wx_ref, cw_ref, cb_ref, wa_ref, ba_ref, wi_ref, bi_ref, lam_ref,
                o_ref, halo_ref, hc_ref, *, tm, width):
    @pl.when(pl.program_id(2) == 0)
    def _():
        halo_ref[...] = jnp.zeros_like(halo_ref)
        hc_ref[...] = jnp.zeros_like(hc_ref)

    a = h_ref[...]
    gate = _mm(a, wg_ref[...])
    xr = _mm(a, wx_ref[...])
    xc = _causal_conv(xr, cw_ref, halo_ref[...], width) + cb_ref[...]
    halo_ref[...] = xr[tm - SUBLANES:, :]
    xcb = xc.astype(BF16)
    r = _sigmoid(_mm(xcb, wa_ref[...]) + ba_ref[...])
    ig = _sigmoid(_mm(xcb, wi_ref[...]) + bi_ref[...])
    log_a = (-LRU_C) * r * _softplus(-lam_ref[...])
    av = jnp.exp(log_a)
    bv = jnp.sqrt(1.0 - jnp.exp(2.0 * log_a)) * (ig * xc)
    row = lax.broadcasted_iota(jnp.int32, av.shape, 0)
    dist = 1
    while dist < tm:
        keep = row >= dist
        a_sh = jnp.where(keep, pltpu.roll(av, dist, 0), 1.0)
        b_sh = jnp.where(keep, pltpu.roll(bv, dist, 0), 0.0)
        bv = av * b_sh + bv
        av = av * a_sh
        dist *= 2
    hs = av * hc_ref[0:1, :] + bv
    hc_ref[0:1, :] = hs[tm - 1:tm, :]
    o_ref[...] = (hs * _gelu_tanh(gate)).astype(o_ref.dtype)


def _lru_in(hn, w_in, conv_w, conv_b, w_a, b_a, w_x, b_x, lam, batch, seq):
    t, d = hn.shape
    width = conv_w.shape[0]
    tm, tn = 512, LRU_BLOCK
    ns, nj = seq // tm, d // tn
    row = lambda j, b, s: (0, j)
    return pl.pallas_call(
        functools.partial(_lru_kernel, tm=tm, width=width),
        grid=(nj, batch, ns),
        in_specs=[
            pl.BlockSpec((tm, d), lambda j, b, s: (b * ns + s, 0)),
            pl.BlockSpec((d, tn), lambda j, b, s: (0, j)),
            pl.BlockSpec((d, tn), lambda j, b, s: (0, j + nj)),
            pl.BlockSpec((width, tn), row),
            pl.BlockSpec((1, tn), row),
            pl.BlockSpec((None, tn, tn), lambda j, b, s: (j, 0, 0)),
            pl.BlockSpec((1, tn), row),
            pl.BlockSpec((None, tn, tn), lambda j, b, s: (j, 0, 0)),
            pl.BlockSpec((1, tn), row),
            pl.BlockSpec((1, tn), row),
        ],
        out_specs=pl.BlockSpec((tm, tn), lambda j, b, s: (b * ns + s, j)),
        out_shape=jax.ShapeDtypeStruct((t, d), BF16),
        scratch_shapes=[pltpu.VMEM((SUBLANES, tn), F32), pltpu.VMEM((SUBLANES, tn), F32)],
        compiler_params=_params(("arbitrary", "arbitrary", "arbitrary")),
        name="lru_in",
    )(hn, w_in, w_in, conv_w, conv_b.reshape(1, d), w_a, b_a.reshape(1, d), w_x, b_x.reshape(1, d),
      lam.reshape(1, d))


def _rope_slot(x, cos_t, sin_t):
    return x * cos_t + pltpu.roll(x, LANES // 2, 1) * sin_t


def _rope_kernel(p_ref, inv_ref, msk_ref, sgn_ref, c_ref, s_ref):
    ang = p_ref[...] * inv_ref[...]
    c_ref[...] = jnp.cos(ang) * msk_ref[...]
    s_ref[...] = jnp.sin(ang) * sgn_ref[...]


def _rope_tables(positions):
    b, s = positions.shape
    t = b * s
    half = MLA_ROPE // 2
    inv = ROPE_BASE ** (-jnp.arange(0, MLA_ROPE, 2, dtype=F32) / MLA_ROPE)
    z = jnp.zeros((half,), F32)
    o = jnp.ones((half,), F32)
    inv_row = jnp.concatenate([inv, z, inv, z]).reshape(1, LANES)
    msk_row = jnp.concatenate([o, z, o, z]).reshape(1, LANES)
    sgn_row = jnp.concatenate([-o, z, o, z]).reshape(1, LANES)
    pos = jnp.broadcast_to(positions.astype(F32).reshape(t, 1), (t, LANES))
    tm = 1024
    rowspec = pl.BlockSpec((1, LANES), lambda i: (0, 0))
    tile = pl.BlockSpec((tm, LANES), lambda i: (i, 0))
    return pl.pallas_call(
        _rope_kernel,
        grid=(t // tm,),
        in_specs=[tile, rowspec, rowspec, rowspec],
        out_specs=[tile, tile],
        out_shape=[jax.ShapeDtypeStruct((t, LANES), F32)] * 2,
        compiler_params=_params(("arbitrary",)),
        name="rope_tables",
    )(pos, inv_row, msk_row, sgn_row)


def _rms_rows(x, g):
    return (x * lax.rsqrt(jnp.mean(x * x, axis=-1, keepdims=True) + EPS)) * g


def _mla_in_kernel(h_ref, wq_ref, wkv_ref, wpe_ref, gq_ref, gkv_ref, q_ref, kv_ref, pe_ref):
    a = h_ref[...]
    q_ref[...] = _rms_rows(_mm(a, wq_ref[...]), gq_ref[...]).astype(q_ref.dtype)
    kv_ref[...] = _rms_rows(_mm(a, wkv_ref[...]), gkv_ref[...]).astype(kv_ref.dtype)
    pe_ref[...] = _mm(a, wpe_ref[...])


def _mla_in(hn, wq, wkv, wpe, gq, gkv):
    t, d = hn.shape
    tm = 512
    full = lambda i: (0, 0)
    return pl.pallas_call(
        _mla_in_kernel,
        grid=(t // tm,),
        in_specs=[
            pl.BlockSpec((tm, d), lambda i: (i, 0)),
            pl.BlockSpec((d, MLA_Q_RANK), full),
            pl.BlockSpec((d, MLA_KV_RANK), full),
            pl.BlockSpec((d, LANES), full),
            pl.BlockSpec((1, MLA_Q_RANK), full),
            pl.BlockSpec((1, MLA_KV_RANK), full),
        ],
        out_specs=[
            pl.BlockSpec((tm, MLA_Q_RANK), lambda i: (i, 0)),
            pl.BlockSpec((tm, MLA_KV_RANK), lambda i: (i, 0)),
            pl.BlockSpec((tm, LANES), lambda i: (i, 0)),
        ],
        out_shape=[
            jax.ShapeDtypeStruct((t, MLA_Q_RANK), BF16),
            jax.ShapeDtypeStruct((t, MLA_KV_RANK), BF16),
            jax.ShapeDtypeStruct((t, LANES), F32),
        ],
        compiler_params=_params(("arbitrary",)),
        name="mla_in",
    )(hn, wq, wkv, wpe, gq.reshape(1, -1), gkv.reshape(1, -1))


def _mla_q_kernel(a_ref, w_ref, g_ref, c_ref, s_ref, o_ref, *, heads):
    acc = _mm(a_ref[...], w_ref[...])
    cos_t, sin_t = c_ref[...], s_ref[...]
    g = g_ref[...]
    outs = []
    for h in range(heads):
        x = acc[:, h * MLA_SLOT:(h + 1) * MLA_SLOT]
        r = lax.rsqrt(jnp.sum(x * x, axis=-1, keepdims=True) * (1.0 / MLA_QK) + EPS)
        xn = (x * r) * g
        outs.append(xn[:, :MLA_NOPE])
        outs.append(_rope_slot(xn[:, MLA_NOPE:], cos_t, sin_t))
    o_ref[...] = jnp.concatenate(outs, axis=1).astype(o_ref.dtype)


def _mla_q(qn, w_uq, g_slot, cos_t, sin_t):
    t, k = qn.shape
    n = w_uq.shape[1]
    tm, heads = 1024, 2
    tn = heads * MLA_SLOT
    return pl.pallas_call(
        functools.partial(_mla_q_kernel, heads=heads),
        grid=(t // tm, n // tn),
        in_specs=[
            pl.BlockSpec((tm, k), lambda i, j: (i, 0)),
            pl.BlockSpec((k, tn), lambda i, j: (0, j)),
            pl.BlockSpec((1, MLA_SLOT), lambda i, j: (0, 0)),
            pl.BlockSpec((tm, LANES), lambda i, j: (i, 0)),
            pl.BlockSpec((tm, LANES), lambda i, j: (i, 0)),
        ],
        out_specs=pl.BlockSpec((tm, tn), lambda i, j: (i, j)),
        out_shape=jax.ShapeDtypeStruct((t, n), BF16),
        compiler_params=_params(("arbitrary", "arbitrary")),
        name="mla_q",
    )(qn, w_uq, g_slot, cos_t, sin_t)


def _mla_kv_kernel(a_ref, wk_ref, wv_ref, pe_ref, g_ref, c_ref, s_ref, k_ref, v_ref, *, heads):
    a = a_ref[...]
    kacc = _mm(a, wk_ref[...])
    v_ref[...] = _mm(a, wv_ref[...]).astype(v_ref.dtype)
    pe = pe_ref[...]
    pe_ss = jnp.sum(pe * pe, axis=-1, keepdims=True)
    cos_t, sin_t = c_ref[...], s_ref[...]
    g = g_ref[...]
    outs = []
    for h in range(heads):
        x = kacc[:, h * MLA_NOPE:(h + 1) * MLA_NOPE]
        r = lax.rsqrt((jnp.sum(x * x, axis=-1, keepdims=True) + pe_ss) * (1.0 / MLA_QK) + EPS)
        outs.append((x * r) * g[:, :MLA_NOPE])
        outs.append(_rope_slot((pe * r) * g[:, MLA_NOPE:], cos_t, sin_t))
    k_ref[...] = jnp.concatenate(outs, axis=1).astype(k_ref.dtype)


def _mla_kv(kvn, wk, wv, pe, g_slot, cos_t, sin_t):
    t, k = kvn.shape
    tm, heads = 1024, 2
    nj = MLA_HEADS // heads
    return pl.pallas_call(
        functools.partial(_mla_kv_kernel, heads=heads),
        grid=(t // tm, nj),
        in_specs=[
            pl.BlockSpec((tm, k), lambda i, j: (i, 0)),
            pl.BlockSpec((k, heads * MLA_NOPE), lambda i, j: (0, j)),
            pl.BlockSpec((k, heads * MLA_V), lambda i, j: (0, j)),
            pl.BlockSpec((tm, LANES), lambda i, j: (i, 0)),
            pl.BlockSpec((1, MLA_SLOT), lambda i, j: (0, 0)),
            pl.BlockSpec((tm, LANES), lambda i, j: (i, 0)),
            pl.BlockSpec((tm, LANES), lambda i, j: (i, 0)),
        ],
        out_specs=[
            pl.BlockSpec((tm, heads * MLA_SLOT), lambda i, j: (i, j)),
            pl.BlockSpec((tm, heads * MLA_V), lambda i, j: (i, j)),
        ],
        out_shape=[
            jax.ShapeDtypeStruct((t, MLA_HEADS * MLA_SLOT), BF16),
            jax.ShapeDtypeStruct((t, MLA_HEADS * MLA_V), BF16),
        ],
        compiler_params=_params(("arbitrary", "arbitrary")),
        name="mla_kv",
    )(kvn, wk, wv, pe, g_slot, cos_t, sin_t)


def _attn_kernel(q_ref, k_ref, v_ref, o_ref, *, seq, tq, scale):
    rch = lax.broadcasted_iota(jnp.int32, (tq, tq), 0) // CHUNK
    cch = lax.broadcasted_iota(jnp.int32, (tq, tq), 1) // CHUNK
    visible = cch <= rch
    for qi in range(seq // tq):
        lo, hi = qi * tq, (qi + 1) * tq
        q = q_ref[lo:hi, :]
        sd = jnp.where(visible, _mm_nt(q, k_ref[lo:hi, :]) * scale, -1e30)
        m = jnp.max(sd, axis=-1, keepdims=True)
        if qi > 0:
            so = _mm_nt(q, k_ref[0:lo, :]) * scale
            m = jnp.maximum(m, jnp.max(so, axis=-1, keepdims=True))
        pd = jnp.exp(sd - m)
        den = jnp.sum(pd, axis=-1, keepdims=True)
        acc = _mm(pd.astype(BF16), v_ref[lo:hi, :])
        if qi > 0:
            po = jnp.exp(so - m)
            den = den + jnp.sum(po, axis=-1, keepdims=True)
            acc = acc + _mm(po.astype(BF16), v_ref[0:lo, :])
        o_ref[lo:hi, :] = (acc / den).astype(o_ref.dtype)


def _attention(q, k, v, batch, seq):
    t = q.shape[0]
    tq = 256
    return pl.pallas_call(
        functools.partial(_attn_kernel, seq=seq, tq=tq, scale=MLA_QK ** -0.5),
        grid=(batch, MLA_HEADS),
        in_specs=[
            pl.BlockSpec((seq, MLA_SLOT), lambda b, h: (b, h)),
            pl.BlockSpec((seq, MLA_SLOT), lambda b, h: (b, h)),
            pl.BlockSpec((seq, MLA_V), lambda b, h: (b, h)),
        ],
        out_specs=pl.BlockSpec((seq, MLA_V), lambda b, h: (b, h)),
        out_shape=jax.ShapeDtypeStruct((t, MLA_HEADS * MLA_V), BF16),
        compiler_params=_params(("arbitrary", "arbitrary")),
        name="mla_attention",
    )(q, k, v)


def _slot_cols(w3):
    half = MLA_ROPE // 2
    z = jnp.zeros(w3.shape[:-1] + (half,), w3.dtype)
    return jnp.concatenate(
        [w3[..., :MLA_NOPE], w3[..., MLA_NOPE:MLA_NOPE + half], z, w3[..., MLA_NOPE + half:], z], axis=-1)


def _mla_mixer(hn, cos_t, sin_t, w_in, q_lat_g, kv_lat_g, w_uq, w_ukv, q_norm_g, k_norm_g, batch, seq):
    d = hn.shape[1]
    half = MLA_ROPE // 2
    wq = w_in[:, :MLA_Q_RANK].astype(BF16)
    wkv = w_in[:, MLA_Q_RANK:MLA_Q_RANK + MLA_KV_RANK].astype(BF16)
    wpe_raw = w_in[:, MLA_Q_RANK + MLA_KV_RANK:]
    z = jnp.zeros((d, half), w_in.dtype)
    wpe = jnp.concatenate([wpe_raw[:, :half], z, wpe_raw[:, half:], z], axis=1).astype(BF16)
    qn, kvn, pe = _mla_in(hn, wq, wkv, wpe, q_lat_g, kv_lat_g)
    w_uq_s = _slot_cols(w_uq.reshape(MLA_Q_RANK, MLA_HEADS, MLA_QK)).reshape(
        MLA_Q_RANK, MLA_HEADS * MLA_SLOT).astype(BF16)
    w_ukv3 = w_ukv.reshape(MLA_KV_RANK, MLA_HEADS, MLA_NOPE + MLA_V)
    wk = w_ukv3[:, :, :MLA_NOPE].reshape(MLA_KV_RANK, MLA_HEADS * MLA_NOPE).astype(BF16)
    wv = w_ukv3[:, :, MLA_NOPE:].reshape(MLA_KV_RANK, MLA_HEADS * MLA_V).astype(BF16)
    gq = _slot_cols(q_norm_g).reshape(1, MLA_SLOT)
    gk = _slot_cols(k_norm_g).reshape(1, MLA_SLOT)
    q = _mla_q(qn, w_uq_s, gq, cos_t, sin_t)
    k, v = _mla_kv(kvn, wk, wv, pe, gk, cos_t, sin_t)
    return _attention(q, k, v, batch, seq)


def _gdn_conv_kernel(h_ref, w_ref, cw_ref, o_ref, halo_ref, *, tm, width, n_norm, n_q):
    @pl.when(pl.program_id(2) == 0)
    def _():
        halo_ref[...] = jnp.zeros_like(halo_ref)

    j = pl.program_id(0)
    z = _mm(h_ref[...], w_ref[...])
    y = _silu(_causal_conv(z, cw_ref, halo_ref[...], width))
    halo_ref[...] = z[tm - SUBLANES:, :]
    qscale = jnp.where(j < n_q, GDN_HEAD ** -0.5, 1.0)
    outs = []
    for g in range(y.shape[1] // GDN_HEAD):
        yg = y[:, g * GDN_HEAD:(g + 1) * GDN_HEAD]
        yn = (yg * lax.rsqrt(jnp.sum(yg * yg, axis=-1, keepdims=True) + EPS)) * qscale
        outs.append(jnp.where(j < n_norm, yn, yg))
    o_ref[...] = jnp.concatenate(outs, axis=1)


def _gdn_conv_in(hn, w_in, conv_w, batch, seq):
    t, d = hn.shape
    width = conv_w.shape[0]
    tm, tn = 1024, 512
    ns, nj = seq // tm, GDN_QKV // tn
    return pl.pallas_call(
        functools.partial(_gdn_conv_kernel, tm=tm, width=width, n_norm=2 * GDN_KD // tn, n_q=GDN_KD // tn),
        grid=(nj, batch, ns),
        in_specs=[
            pl.BlockSpec((tm, d), lambda j, b, s: (b * ns + s, 0)),
            pl.BlockSpec((d, tn), lambda j, b, s: (0, j)),
            pl.BlockSpec((width, tn), lambda j, b, s: (0, j)),
        ],
        out_specs=pl.BlockSpec((tm, tn), lambda j, b, s: (b * ns + s, j)),
        out_shape=jax.ShapeDtypeStruct((t, GDN_QKV), F32),
        scratch_shapes=[pltpu.VMEM((SUBLANES, tn), F32)],
        compiler_params=_params(("arbitrary", "arbitrary", "arbitrary")),
        name="gdn_conv_in",
    )(hn, w_in, conv_w)


def _gdn_gates_kernel(ba_ref, alog_ref, dt_ref, beta_ref, gc_ref):
    ba = ba_ref[...]
    beta_ref[...] = _sigmoid(ba)
    g = -jnp.exp(alog_ref[...]) * _softplus(ba + dt_ref[...])
    rin = lax.broadcasted_iota(jnp.int32, g.shape, 0) % CHUNK
    dist = 1
    while dist < CHUNK:
        g = g + jnp.where(rin >= dist, pltpu.roll(g, dist, 0), 0.0)
        dist *= 2
    gc_ref[...] = g


def _gdn_gates(ba, a_log, dt_bias):
    t = ba.shape[0]
    pad = jnp.zeros((GDN_V_HEADS,), F32)
    tail = jnp.zeros((LANES - 2 * GDN_V_HEADS,), F32)
    alog_row = jnp.concatenate([pad, a_log.astype(F32), tail]).reshape(1, LANES)
    dt_row = jnp.concatenate([pad, dt_bias.astype(F32), tail]).reshape(1, LANES)
    tm = 512
    tile = pl.BlockSpec((tm, LANES), lambda i: (i, 0))
    rowspec = pl.BlockSpec((1, LANES), lambda i: (0, 0))
    return pl.pallas_call(
        _gdn_gates_kernel,
        grid=(t // tm,),
        in_specs=[tile, rowspec, rowspec],
        out_specs=[tile, tile],
        out_shape=[jax.ShapeDtypeStruct((t, LANES), F32)] * 2,
        compiler_params=_params(("arbitrary",)),
        name="gdn_gates",
    )(ba, alog_row, dt_row)


def _pair_mask(ri, ci, s):
    sh = s.bit_length() - 1
    return (((ri >> (sh + 1)) == (ci >> (sh + 1))) & (((ri >> sh) & 1) == 1) & (((ci >> sh) & 1) == 0))


def _lane_col(x, idx):
    lane = lax.broadcasted_iota(jnp.int32, x.shape, 1)
    return jnp.sum(jnp.where(lane == idx, x, 0.0), axis=1, keepdims=True)


def _gdn_core_kernel(q_ref, k_ref, v_ref, z_ref, beta_ref, gc_ref, gn_ref, o_ref,
                     u_s, w_s, qg_s, kg_s, a_s, eg_s, o_s, *, seq, rep):
    n_chunks = seq // CHUNK
    pair = pl.program_id(1)
    ri = lax.broadcasted_iota(jnp.int32, (CHUNK, CHUNK), 0)
    ci = lax.broadcasted_iota(jnp.int32, (CHUNK, CHUNK), 1)
    eye, tril = ri == ci, ci <= ri

    def prep(c, carry):
        rows = pl.ds(pl.multiple_of(c * CHUNK, CHUNK), CHUNK)
        k = k_ref[rows, :]
        q = q_ref[rows, :]
        kb16 = k.astype(BF16)
        qk = _mm_nt(q.astype(BF16), kb16)
        beta_all = beta_ref[rows, :]
        gc_all = gc_ref[rows, :]
        for hh in range(rep):
            hv = pair * rep + hh
            b = _lane_col(beta_all, hv)
            g = _lane_col(gc_all, GDN_V_HEADS + hv)
            gb = jnp.broadcast_to(g, (CHUNK, CHUNK))
            grow = jnp.sum(jnp.where(eye, gb, 0.0), axis=0, keepdims=True)
            decay = jnp.where(tril, jnp.exp(jnp.where(tril, gb - grow, 0.0)), 0.0)
            kb = k * b
            mm_ = _mm_nt(kb.astype(BF16), kb16) * decay
            x = jnp.where(_pair_mask(ri, ci, 1), -mm_, 0.0)
            s = 2
            while s < CHUNK:
                m_off = jnp.where(_pair_mask(ri, ci, s), mm_, 0.0)
                x16 = x.astype(BF16)
                y = m_off + _mm(x16, m_off.astype(BF16))
                x = x - (y + _mm(y.astype(BF16), x16))
                s *= 2
            eg = jnp.exp(g)
            v = v_ref[rows, hh * GDN_HEAD:(hh + 1) * GDN_HEAD]
            vb = v * b
            kbg = kb * eg
            x16 = x.astype(BF16)
            cols = slice(hh * GDN_HEAD, (hh + 1) * GDN_HEAD)
            u_s[rows, cols] = vb + _mm(x16, vb.astype(BF16))
            w_s[rows, cols] = kbg + _mm(x16, kbg.astype(BF16))
            glast = g[CHUNK - 1:CHUNK, :]
            qg_s[rows, cols] = q * eg
            kg_s[rows, cols] = k * jnp.exp(glast - g)
            a_s[rows, hh * CHUNK:(hh + 1) * CHUNK] = qk * decay
            eg_s[pl.ds(pl.multiple_of(c * SUBLANES, SUBLANES), SUBLANES), cols] = jnp.broadcast_to(
                jnp.exp(glast), (SUBLANES, GDN_HEAD))
        return carry

    lax.fori_loop(0, n_chunks, prep, 0)

    def step(c, states):
        rows = pl.ds(pl.multiple_of(c * CHUNK, CHUNK), CHUNK)
        new_states = []
        for hh in range(rep):
            st = states[hh]
            cols = slice(hh * GDN_HEAD, (hh + 1) * GDN_HEAD)
            st16 = st.astype(BF16)
            v_new = u_s[rows, cols] - _mm(w_s[rows, cols].astype(BF16), st16)
            vn16 = v_new.astype(BF16)
            o_s[rows, cols] = (_mm(qg_s[rows, cols].astype(BF16), st16)
                               + _mm(a_s[rows, hh * CHUNK:(hh + 1) * CHUNK].astype(BF16), vn16))
            dec = eg_s[pl.ds(pl.multiple_of(c * SUBLANES, SUBLANES), 1), cols]
            new_states.append(st * dec + _mm_tn(kg_s[rows, cols].astype(BF16), vn16))
        return tuple(new_states)

    init = tuple(jnp.zeros((GDN_HEAD, GDN_HEAD), F32) for _ in range(rep))
    lax.fori_loop(0, n_chunks, step, init)

    for hh in range(rep):
        cols = slice(hh * GDN_HEAD, (hh + 1) * GDN_HEAD)
        o = o_s[:, cols]
        on = (o * lax.rsqrt(jnp.mean(o * o, axis=-1, keepdims=True) + EPS)) * gn_ref[...]
        o_ref[:, cols] = (on * _silu(z_ref[:, cols])).astype(o_ref.dtype)


def _gdn_core(qkv, z, beta, gc, o_norm_g, batch, seq):
    t = qkv.shape[0]
    rep = GDN_V_HEADS // GDN_K_HEADS
    wide = rep * GDN_HEAD
    koff = GDN_KD // GDN_HEAD
    voff = 2 * GDN_KD // wide
    f32buf = lambda cols: pltpu.VMEM((seq, cols), F32)
    return pl.pallas_call(
        functools.partial(_gdn_core_kernel, seq=seq, rep=rep),
        grid=(batch, GDN_K_HEADS),
        in_specs=[
            pl.BlockSpec((seq, GDN_HEAD), lambda b, p: (b, p)),
            pl.BlockSpec((seq, GDN_HEAD), lambda b, p: (b, p + koff)),
            pl.BlockSpec((seq, wide), lambda b, p: (b, p + voff)),
            pl.BlockSpec((seq, wide), lambda b, p: (b, p)),
            pl.BlockSpec((seq, LANES), lambda b, p: (b, 0)),
            pl.BlockSpec((seq, LANES), lambda b, p: (b, 0)),
            pl.BlockSpec((1, GDN_HEAD), lambda b, p: (0, 0)),
        ],
        out_specs=pl.BlockSpec((seq, wide), lambda b, p: (b, p)),
        out_shape=jax.ShapeDtypeStruct((t, GDN_VD), BF16),
        scratch_shapes=[f32buf(wide), f32buf(wide), f32buf(wide), f32buf(wide),
                        f32buf(rep * CHUNK), pltpu.VMEM((seq // CHUNK * SUBLANES, wide), F32),
                        f32buf(wide)],
        compiler_params=_params(("arbitrary", "arbitrary")),
        name="gdn_core",
    )(qkv, qkv, qkv, z, beta, gc, o_norm_g.reshape(1, GDN_HEAD))


def _gdn_mixer(hn, w_in, conv_w, a_log, dt_bias, o_norm_g, batch, seq):
    w16 = w_in.astype(BF16)
    qkv = _gdn_conv_in(hn, w16, conv_w, batch, seq)
    z = _matmul(hn, w16, GDN_VD, GDN_QKV, 512, F32)
    w_ba = w_in[:, GDN_QKV + GDN_VD:]
    w_ba = jnp.concatenate([w_ba, jnp.zeros((w_ba.shape[0], LANES - w_ba.shape[1]), w_ba.dtype)], axis=1)
    ba = _matmul(hn, w_ba.astype(BF16), LANES, 0, LANES, F32)
    beta, gc = _gdn_gates(ba, a_log, dt_bias)
    return _gdn_core(qkv, z, beta, gc, o_norm_g, batch, seq)


def kernel(x, c, positions, cond_w, cond_b, mod_w, mod_b, norm_g, ffn_w_in, ffn_w_out, sc_w_in, sc_conv_w, sc_w_out, mla_w_in, mla_q_lat_g, mla_kv_lat_g, mla_w_uq, mla_w_ukv, mla_q_norm_g, mla_k_norm_g, mla_w_o, lru_w_in, lru_conv_w, lru_conv_b, lru_w_a, lru_b_a, lru_w_x, lru_b_x, lru_lam, lru_w_out, gdn_w_in, gdn_conv_w, gdn_a_log, gdn_dt_bias, gdn_o_norm_g, gdn_w_out):
    batch, seq, d = x.shape
    depth = mod_w.shape[0]
    n_mixers = 4
    x2 = x.reshape(batch * seq, d)
    mod = _modulation(c, cond_w, cond_b, mod_w, mod_b).reshape(depth, batch, 3, 3, d)
    cos_t = sin_t = None

    def ffn(x2, i, sub, which):
        hn = _prenorm(x2, norm_g[i, sub], mod[i, :, sub, 1], mod[i, :, sub, 0], seq)
        hid = _swiglu_in(hn, ffn_w_in[i, which].astype(BF16))
        alpha = 0.5 * (1.0 + mod[i, :, sub, 2])
        return _matmul_residual(hid, ffn_w_out[i, which].astype(BF16), x2, alpha, seq)

    for i in range(depth):
        m, j = i % n_mixers, i // n_mixers
        x2 = ffn(x2, i, 0, 0)
        hn = _prenorm(x2, norm_g[i, 1], mod[i, :, 1, 1], mod[i, :, 1, 0], seq)
        if m == 0:
            y = _sconv_in(hn, sc_w_in[j].astype(BF16), sc_conv_w[j], batch, seq)
            w_out = sc_w_out[j]
        elif m == 1:
            if cos_t is None:
                cos_t, sin_t = _rope_tables(positions)
            y = _mla_mixer(hn, cos_t, sin_t, mla_w_in[j], mla_q_lat_g[j], mla_kv_lat_g[j], mla_w_uq[j],
                           mla_w_ukv[j], mla_q_norm_g[j], mla_k_norm_g[j], batch, seq)
            w_out = mla_w_o[j]
        elif m == 2:
            y = _lru_in(hn, lru_w_in[j].astype(BF16), lru_conv_w[j], lru_conv_b[j], lru_w_a[j].astype(BF16),
                        lru_b_a[j], lru_w_x[j].astype(BF16), lru_b_x[j], lru_lam[j], batch, seq)
            w_out = lru_w_out[j]
        else:
            y = _gdn_mixer(hn, gdn_w_in[j], gdn_conv_w[j], gdn_a_log[j], gdn_dt_bias[j], gdn_o_norm_g[j],
                           batch, seq)
            w_out = gdn_w_out[j]
        x2 = _matmul_residual(y, w_out.astype(BF16), x2, 1.0 + mod[i, :, 1, 2], seq)
        x2 = ffn(x2, i, 2, 1)
    return x2.reshape(batch, seq, d)
```

```python
import functools

import jax
import jax.numpy as jnp
import numpy as np
from jax import lax
from jax.experimental import pallas as pl
from jax.experimental.pallas import tpu as pltpu

F32 = jnp.float32
BF16 = jnp.bfloat16

EPS = 1e-6
CHUNK = 64
MLA_HEADS = 32
MLA_Q_RANK = 1024
MLA_KV_RANK = 512
MLA_NOPE = 128
MLA_ROPE = 64
MLA_V = 128
MLA_QK = MLA_NOPE + MLA_ROPE
MLA_SLOT = 256
ROPE_BASE = 10000.0
LRU_BLOCK = 256
LRU_C = 8.0
GDN_K_HEADS = 16
GDN_V_HEADS = 32
GDN_HEAD = 128
GDN_KD = GDN_K_HEADS * GDN_HEAD
GDN_VD = GDN_V_HEADS * GDN_HEAD
GDN_QKV = 2 * GDN_KD + GDN_VD

LANES = 128
SUBLANES = 8
VMEM_LIMIT = 56 * 1024 * 1024


def _params(sem):
    return pltpu.CompilerParams(dimension_semantics=sem, vmem_limit_bytes=VMEM_LIMIT)


def _mm(a, b):
    return jnp.dot(a, b, preferred_element_type=F32)


def _mm_nt(a, b):
    return lax.dot_general(a, b, (((1,), (1,)), ((), ())), preferred_element_type=F32)


def _mm_tn(a, b):
    return lax.dot_general(a, b, (((0,), (0,)), ((), ())), preferred_element_type=F32)


def _sigmoid(x):
    return 1.0 / (1.0 + jnp.exp(-x))


def _silu(x):
    return x * _sigmoid(x)


def _softplus(x):
    return jnp.maximum(x, 0.0) + jnp.log(1.0 + jnp.exp(-jnp.abs(x)))


def _gelu_tanh(x):
    c = np.sqrt(2.0 / np.pi).astype(np.float32)
    return 0.5 * x * (1.0 + jnp.tanh(c * (x + 0.044715 * (x * x * x))))


def _mod_kernel(c_ref, cw_ref, cb_ref, mw_ref, mb_ref, o_ref):
    cond = _silu(_mm(c_ref[...].astype(BF16), cw_ref[...].astype(BF16)) + cb_ref[...])
    o_ref[...] = _mm(cond.astype(BF16), mw_ref[...].astype(BF16)) + mb_ref[...]


def _modulation(c, cond_w, cond_b, mod_w, mod_b):
    depth, d_cond, n = mod_w.shape
    b, d = c.shape
    tn = d
    return pl.pallas_call(
        _mod_kernel,
        grid=(depth, n // tn),
        in_specs=[
            pl.BlockSpec((b, d), lambda i, j: (0, 0)),
            pl.BlockSpec((d, d_cond), lambda i, j: (0, 0)),
            pl.BlockSpec((1, d_cond), lambda i, j: (0, 0)),
            pl.BlockSpec((None, d_cond, tn), lambda i, j: (i, 0, j)),
            pl.BlockSpec((None, 1, tn), lambda i, j: (i, 0, j)),
        ],
        out_specs=pl.BlockSpec((None, b, tn), lambda i, j: (i, 0, j)),
        out_shape=jax.ShapeDtypeStruct((depth, b, n), F32),
        compiler_params=_params(("arbitrary", "arbitrary")),
        name="modulation",
    )(c, cond_w, cond_b.reshape(1, d_cond), mod_w, mod_b.reshape(depth, 1, n))


def _prenorm_kernel(x_ref, g_ref, sc_ref, sh_ref, o_ref):
    x = x_ref[...]
    y = x * lax.rsqrt(jnp.mean(x * x, axis=-1, keepdims=True) + EPS)
    o_ref[...] = ((y * g_ref[...]) * (1.0 + sc_ref[...]) + sh_ref[...]).astype(o_ref.dtype)


def _prenorm(x2, g, scale, shift, seq):
    t, d = x2.shape
    tm = 256
    tpb = seq // tm
    return pl.pallas_call(
        _prenorm_kernel,
        grid=(t // tm,),
        in_specs=[
            pl.BlockSpec((tm, d), lambda i: (i, 0)),
            pl.BlockSpec((1, d), lambda i: (0, 0)),
            pl.BlockSpec((None, 1, d), lambda i: (i // tpb, 0, 0)),
            pl.BlockSpec((None, 1, d), lambda i: (i // tpb, 0, 0)),
        ],
        out_specs=pl.BlockSpec((tm, d), lambda i: (i, 0)),
        out_shape=jax.ShapeDtypeStruct((t, d), BF16),
        compiler_params=_params(("arbitrary",)),
        name="prenorm",
    )(x2, g.reshape(1, d), scale[:, None, :], shift[:, None, :])


def _swiglu_kernel(h_ref, wg_ref, wu_ref, o_ref):
    a = h_ref[...]
    gt = _mm(a, wg_ref[...])
    up = _mm(a, wu_ref[...])
    o_ref[...] = (_silu(gt) * up).astype(o_ref.dtype)


def _swiglu_in(hn, w_in):
    t, d = hn.shape
    f = w_in.shape[1] // 2
    tm, tn = 1024, 512
    nj = f // tn
    return pl.pallas_call(
        _swiglu_kernel,
        grid=(t // tm, nj),
        in_specs=[
            pl.BlockSpec((tm, d), lambda i, j: (i, 0)),
            pl.BlockSpec((d, tn), lambda i, j: (0, j)),
            pl.BlockSpec((d, tn), lambda i, j: (0, j + nj)),
        ],
        out_specs=pl.BlockSpec((tm, tn), lambda i, j: (i, j)),
        out_shape=jax.ShapeDtypeStruct((t, f), BF16),
        compiler_params=_params(("arbitrary", "arbitrary")),
        name="swiglu_in",
    )(hn, w_in, w_in)


def _mm_res_kernel(a_ref, w_ref, r_ref, al_ref, o_ref):
    o_ref[...] = r_ref[...] + al_ref[...] * _mm(a_ref[...], w_ref[...])


def _matmul_residual(a, w, res, alpha, seq):
    t, k = a.shape
    n = w.shape[1]
    tm, tn = 1024, 512
    tpb = seq // tm
    return pl.pallas_call(
        _mm_res_kernel,
        grid=(t // tm, n // tn),
        in_specs=[
            pl.BlockSpec((tm, k), lambda i, j: (i, 0)),
            pl.BlockSpec((k, tn), lambda i, j: (0, j)),
            pl.BlockSpec((tm, tn), lambda i, j: (i, j)),
            pl.BlockSpec((None, 1, tn), lambda i, j: (i // tpb, 0, j)),
        ],
        out_specs=pl.BlockSpec((tm, tn), lambda i, j: (i, j)),
        out_shape=jax.ShapeDtypeStruct((t, n), F32),
        compiler_params=_params(("arbitrary", "arbitrary")),
        name="matmul_residual",
    )(a, w, res, alpha[:, None, :])


def _mm_kernel(a_ref, w_ref, o_ref):
    o_ref[...] = _mm(a_ref[...], w_ref[...]).astype(o_ref.dtype)


def _matmul(a, w, n, col_off, tn, out_dtype):
    t, k = a.shape
    tm = 1024
    joff = col_off // tn
    return pl.pallas_call(
        _mm_kernel,
        grid=(t // tm, n // tn),
        in_specs=[
            pl.BlockSpec((tm, k), lambda i, j: (i, 0)),
            pl.BlockSpec((k, tn), lambda i, j: (0, j + joff)),
        ],
        out_specs=pl.BlockSpec((tm, tn), lambda i, j: (i, j)),
        out_shape=jax.ShapeDtypeStruct((t, n), out_dtype),
        compiler_params=_params(("arbitrary", "arbitrary")),
        name="matmul",
    )(a, w)


def _shift_rows(z, halo, k, row8):
    zk = pltpu.roll(z, k, 0)
    hk = pltpu.roll(halo, k, 0)
    top = jnp.where(row8 < k, hk, zk[:SUBLANES])
    return jnp.concatenate([top, zk[SUBLANES:]], axis=0)


def _causal_conv(z, cw_ref, halo, width):
    row8 = lax.broadcasted_iota(jnp.int32, (SUBLANES, z.shape[1]), 0)
    acc = z * cw_ref[width - 1:width, :]
    for k in range(1, width):
        acc = acc + _shift_rows(z, halo, k, row8) * cw_ref[width - 1 - k:width - k, :]
    return acc


def _sconv_kernel(h_ref, wb_ref, wc_ref, wx_ref, cw_ref, o_ref, halo_ref, *, tm, width):
    @pl.when(pl.program_id(2) == 0)
    def _():
        halo_ref[...] = jnp.zeros_like(halo_ref)

    a = h_ref[...]
    bg = _mm(a, wb_ref[...])
    z = _mm(a, wc_ref[...]) * _mm(a, wx_ref[...])
    conv = _causal_conv(z, cw_ref, halo_ref[...], width)
    halo_ref[...] = z[tm - SUBLANES:, :]
    o_ref[...] = (bg * conv).astype(o_ref.dtype)


def _sconv_in(hn, w_in, conv_w, batch, seq):
    t, d = hn.shape
    width = conv_w.shape[0]
    tm, tn = 1024, 256
    ns, nj = seq // tm, d // tn
    return pl.pallas_call(
        functools.partial(_sconv_kernel, tm=tm, width=width),
        grid=(nj, batch, ns),
        in_specs=[
            pl.BlockSpec((tm, d), lambda j, b, s: (b * ns + s, 0)),
            pl.BlockSpec((d, tn), lambda j, b, s: (0, j)),
            pl.BlockSpec((d, tn), lambda j, b, s: (0, j + nj)),
            pl.BlockSpec((d, tn), lambda j, b, s: (0, j + 2 * nj)),
            pl.BlockSpec((width, tn), lambda j, b, s: (0, j)),
        ],
        out_specs=pl.BlockSpec((tm, tn), lambda j, b, s: (b * ns + s, j)),
        out_shape=jax.ShapeDtypeStruct((t, d), BF16),
        scratch_shapes=[pltpu.VMEM((SUBLANES, tn), F32)],
        compiler_params=_params(("arbitrary", "arbitrary", "arbitrary")),
        name="sconv_in",
    )(hn, w_in, w_in, w_in, conv_w)


def _lru_kernel(h_ref, wg_ref, wx_ref, cw_ref, cb_ref, wa_ref, ba_ref, wi_ref, bi_ref, lam_ref,
                o_ref, halo_ref, hc_ref, *, tm, width):
    @pl.when(pl.program_id(2) == 0)
    def _():
        halo_ref[...] = jnp.zeros_like(halo_ref)
        hc_ref[...] = jnp.zeros_like(hc_ref)

    a = h_ref[...]
    gate = _mm(a, wg_ref[...])
    xr = _mm(a, wx_ref[...])
    xc = _causal_conv(xr, cw_ref, halo_ref[...], width) + cb_ref[...]
    halo_ref[...] = xr[tm - SUBLANES:, :]
    xcb = xc.astype(BF16)
    r = _sigmoid(_mm(xcb, wa_ref[...]) + ba_ref[...])
    ig = _sigmoid(_mm(xcb, wi_ref[...]) + bi_ref[...])
    log_a = (-LRU_C) * r * _softplus(-lam_ref[...])
    av = jnp.exp(log_a)
    bv = jnp.sqrt(1.0 - jnp.exp(2.0 * log_a)) * (ig * xc)
    row = lax.broadcasted_iota(jnp.int32, av.shape, 0)
    dist = 1
    while dist < tm:
        keep = row >= dist
        a_sh = jnp.where(keep, pltpu.roll(av, dist, 0), 1.0)
        b_sh = jnp.where(keep, pltpu.roll(bv, dist, 0), 0.0)
        bv = av * b_sh + bv
        av = av * a_sh
        dist *= 2
    hs = av * hc_ref[0:1, :] + bv
    hc_ref[0:1, :] = hs[tm - 1:tm, :]
    o_ref[...] = (hs * _gelu_tanh(gate)).astype(o_ref.dtype)


def _lru_in(hn, w_in, conv_w, conv_b, w_a, b_a, w_x, b_x, lam, batch, seq):
    t, d = hn.shape
    width = conv_w.shape[0]
    tm, tn = 512, LRU_BLOCK
    ns, nj = seq // tm, d // tn
    row = lambda j, b, s: (0, j)
    return pl.pallas_call(
        functools.partial(_lru_kernel, tm=tm, width=width),
        grid=(nj, batch, ns),
        in_specs=[
            pl.BlockSpec((tm, d), lambda j, b, s: (b * ns + s, 0)),
            pl.BlockSpec((d, tn), lambda j, b, s: (0, j)),
            pl.BlockSpec((d, tn), lambda j, b, s: (0, j + nj)),
            pl.BlockSpec((width, tn), row),
            pl.BlockSpec((1, tn), row),
            pl.BlockSpec((None, tn, tn), lambda j, b, s: (j, 0, 0)),
            pl.BlockSpec((1, tn), row),
            pl.BlockSpec((None, tn, tn), lambda j, b, s: (j, 0, 0)),
            pl.BlockSpec((1, tn), row),
            pl.BlockSpec((1, tn), row),
        ],
        out_specs=pl.BlockSpec((tm, tn), lambda j, b, s: (b * ns + s, j)),
        out_shape=jax.ShapeDtypeStruct((t, d), BF16),
        scratch_shapes=[pltpu.VMEM((SUBLANES, tn), F32), pltpu.VMEM((SUBLANES, tn), F32)],
        compiler_params=_params(("arbitrary", "arbitrary", "arbitrary")),
        name="lru_in",
    )(hn, w_in, w_in, conv_w, conv_b.reshape(1, d), w_a, b_a.reshape(1, d), w_x, b_x.reshape(1, d),
      lam.reshape(1, d))


def _rope_slot(x, cos_t, sin_t):
    return x * cos_t + pltpu.roll(x, LANES // 2, 1) * sin_t


def _rope_kernel(p_ref, inv_ref, msk_ref, sgn_ref, c_ref, s_ref):
    ang = p_ref[...] * inv_ref[...]
    c_ref[...] = jnp.cos(ang) * msk_ref[...]
    s_ref[...] = jnp.sin(ang) * sgn_ref[...]


def _rope_tables(positions):
    b, s = positions.shape
    t = b * s
    half = MLA_ROPE // 2
    inv = ROPE_BASE ** (-jnp.arange(0, MLA_ROPE, 2, dtype=F32) / MLA_ROPE)
    z = jnp.zeros((half,), F32)
    o = jnp.ones((half,), F32)
    inv_row = jnp.concatenate([inv, z, inv, z]).reshape(1, LANES)
    msk_row = jnp.concatenate([o, z, o, z]).reshape(1, LANES)
    sgn_row = jnp.concatenate([-o, z, o, z]).reshape(1, LANES)
    pos = jnp.broadcast_to(positions.astype(F32).reshape(t, 1), (t, LANES))
    tm = 1024
    rowspec = pl.BlockSpec((1, LANES), lambda i: (0, 0))
    tile = pl.BlockSpec((tm, LANES), lambda i: (i, 0))
    return pl.pallas_call(
        _rope_kernel,
        grid=(t // tm,),
        in_specs=[tile, rowspec, rowspec, rowspec],
        out_specs=[tile, tile],
        out_shape=[jax.ShapeDtypeStruct((t, LANES), F32)] * 2,
        compiler_params=_params(("arbitrary",)),
        name="rope_tables",
    )(pos, inv_row, msk_row, sgn_row)


def _rms_rows(x, g):
    return (x * lax.rsqrt(jnp.mean(x * x, axis=-1, keepdims=True) + EPS)) * g


def _mla_in_kernel(h_ref, wq_ref, wkv_ref, wpe_ref, gq_ref, gkv_ref, q_ref, kv_ref, pe_ref):
    a = h_ref[...]
    q_ref[...] = _rms_rows(_mm(a, wq_ref[...]), gq_ref[...]).astype(q_ref.dtype)
    kv_ref[...] = _rms_rows(_mm(a, wkv_ref[...]), gkv_ref[...]).astype(kv_ref.dtype)
    pe_ref[...] = _mm(a, wpe_ref[...])


def _mla_in(hn, wq, wkv, wpe, gq, gkv):
    t, d = hn.shape
    tm = 512
    full = lambda i: (0, 0)
    return pl.pallas_call(
        _mla_in_kernel,
        grid=(t // tm,),
        in_specs=[
            pl.BlockSpec((tm, d), lambda i: (i, 0)),
            pl.BlockSpec((d, MLA_Q_RANK), full),
            pl.BlockSpec((d, MLA_KV_RANK), full),
            pl.BlockSpec((d, LANES), full),
            pl.BlockSpec((1, MLA_Q_RANK), full),
            pl.BlockSpec((1, MLA_KV_RANK), full),
        ],
        out_specs=[
            pl.BlockSpec((tm, MLA_Q_RANK), lambda i: (i, 0)),
            pl.BlockSpec((tm, MLA_KV_RANK), lambda i: (i, 0)),
            pl.BlockSpec((tm, LANES), lambda i: (i, 0)),
        ],
        out_shape=[
            jax.ShapeDtypeStruct((t, MLA_Q_RANK), BF16),
            jax.ShapeDtypeStruct((t, MLA_KV_RANK), BF16),
            jax.ShapeDtypeStruct((t, LANES), F32),
        ],
        compiler_params=_params(("arbitrary",)),
        name="mla_in",
    )(hn, wq, wkv, wpe, gq.reshape(1, -1), gkv.reshape(1, -1))


def _mla_q_kernel(a_ref, w_ref, g_ref, c_ref, s_ref, o_ref, *, heads):
    acc = _mm(a_ref[...], w_ref[...])
    cos_t, sin_t = c_ref[...], s_ref[...]
    g = g_ref[...]
    outs = []
    for h in range(heads):
        x = acc[:, h * MLA_SLOT:(h + 1) * MLA_SLOT]
        r = lax.rsqrt(jnp.sum(x * x, axis=-1, keepdims=True) * (1.0 / MLA_QK) + EPS)
        xn = (x * r) * g
        outs.append(xn[:, :MLA_NOPE])
        outs.append(_rope_slot(xn[:, MLA_NOPE:], cos_t, sin_t))
    o_ref[...] = jnp.concatenate(outs, axis=1).astype(o_ref.dtype)


def _mla_q(qn, w_uq, g_slot, cos_t, sin_t):
    t, k = qn.shape
    n = w_uq.shape[1]
    tm, heads = 1024, 2
    tn = heads * MLA_SLOT
    return pl.pallas_call(
        functools.partial(_mla_q_kernel, heads=heads),
        grid=(t // tm, n // tn),
        in_specs=[
            pl.BlockSpec((tm, k), lambda i, j: (i, 0)),
            pl.BlockSpec((k, tn), lambda i, j: (0, j)),
            pl.BlockSpec((1, MLA_SLOT), lambda i, j: (0, 0)),
            pl.BlockSpec((tm, LANES), lambda i, j: (i, 0)),
            pl.BlockSpec((tm, LANES), lambda i, j: (i, 0)),
        ],
        out_specs=pl.BlockSpec((tm, tn), lambda i, j: (i, j)),
        out_shape=jax.ShapeDtypeStruct((t, n), BF16),
        compiler_params=_params(("arbitrary", "arbitrary")),
        name="mla_q",
    )(qn, w_uq, g_slot, cos_t, sin_t)


def _mla_kv_kernel(a_ref, wk_ref, wv_ref, pe_ref, g_ref, c_ref, s_ref, k_ref, v_ref, *, heads):
    a = a_ref[...]
    kacc = _mm(a, wk_ref[...])
    v_ref[...] = _mm(a, wv_ref[...]).astype(v_ref.dtype)
    pe = pe_ref[...]
    pe_ss = jnp.sum(pe * pe, axis=-1, keepdims=True)
    cos_t, sin_t = c_ref[...], s_ref[...]
    g = g_ref[...]
    outs = []
    for h in range(heads):
        x = kacc[:, h * MLA_NOPE:(h + 1) * MLA_NOPE]
        r = lax.rsqrt((jnp.sum(x * x, axis=-1, keepdims=True) + pe_ss) * (1.0 / MLA_QK) + EPS)
        outs.append((x * r) * g[:, :MLA_NOPE])
        outs.append(_rope_slot((pe * r) * g[:, MLA_NOPE:], cos_t, sin_t))
    k_ref[...] = jnp.concatenate(outs, axis=1).astype(k_ref.dtype)


def _mla_kv(kvn, wk, wv, pe, g_slot, cos_t, sin_t):
    t, k = kvn.shape
    tm, heads = 1024, 2
    nj = MLA_HEADS // heads
    return pl.pallas_call(
        functools.partial(_mla_kv_kernel, heads=heads),
        grid=(t // tm, nj),
        in_specs=[
            pl.BlockSpec((tm, k), lambda i, j: (i, 0)),
            pl.BlockSpec((k, heads * MLA_NOPE), lambda i, j: (0, j)),
            pl.BlockSpec((k, heads * MLA_V), lambda i, j: (0, j)),
            pl.BlockSpec((tm, LANES), lambda i, j: (i, 0)),
            pl.BlockSpec((1, MLA_SLOT), lambda i, j: (0, 0)),
            pl.BlockSpec((tm, LANES), lambda i, j: (i, 0)),
            pl.BlockSpec((tm, LANES), lambda i, j: (i, 0)),
        ],
        out_specs=[
            pl.BlockSpec((tm, heads * MLA_SLOT), lambda i, j: (i, j)),
            pl.BlockSpec((tm, heads * MLA_V), lambda i, j: (i, j)),
        ],
        out_shape=[
            jax.ShapeDtypeStruct((t, MLA_HEADS * MLA_SLOT), BF16),
            jax.ShapeDtypeStruct((t, MLA_HEADS * MLA_V), BF16),
        ],
        compiler_params=_params(("arbitrary", "arbitrary")),
        name="mla_kv",
    )(kvn, wk, wv, pe, g_slot, cos_t, sin_t)


def _attn_kernel(q_ref, k_ref, v_ref, o_ref, *, seq, tq, scale):
    rch = lax.broadcasted_iota(jnp.int32, (tq, tq), 0) // CHUNK
    cch = lax.broadcasted_iota(jnp.int32, (tq, tq), 1) // CHUNK
    visible = cch <= rch
    for qi in range(seq // tq):
        lo, hi = qi * tq, (qi + 1) * tq
        q = q_ref[lo:hi, :]
        sd = jnp.where(visible, _mm_nt(q, k_ref[lo:hi, :]) * scale, -1e30)
        m = jnp.max(sd, axis=-1, keepdims=True)
        if qi > 0:
            so = _mm_nt(q, k_ref[0:lo, :]) * scale
            m = jnp.maximum(m, jnp.max(so, axis=-1, keepdims=True))
        pd = jnp.exp(sd - m)
        den = jnp.sum(pd, axis=-1, keepdims=True)
        acc = _mm(pd.astype(BF16), v_ref[lo:hi, :])
        if qi > 0:
            po = jnp.exp(so - m)
            den = den + jnp.sum(po, axis=-1, keepdims=True)
            acc = acc + _mm(po.astype(BF16), v_ref[0:lo, :])
        o_ref[lo:hi, :] = (acc / den).astype(o_ref.dtype)


def _attention(q, k, v, batch, seq):
    t = q.shape[0]
    tq = 256
    return pl.pallas_call(
        functools.partial(_attn_kernel, seq=seq, tq=tq, scale=MLA_QK ** -0.5),
        grid=(batch, MLA_HEADS),
        in_specs=[
            pl.BlockSpec((seq, MLA_SLOT), lambda b, h: (b, h)),
            pl.BlockSpec((seq, MLA_SLOT), lambda b, h: (b, h)),
            pl.BlockSpec((seq, MLA_V), lambda b, h: (b, h)),
        ],
        out_specs=pl.BlockSpec((seq, MLA_V), lambda b, h: (b, h)),
        out_shape=jax.ShapeDtypeStruct((t, MLA_HEADS * MLA_V), BF16),
        compiler_params=_params(("arbitrary", "arbitrary")),
        name="mla_attention",
    )(q, k, v)


def _slot_cols(w3):
    half = MLA_ROPE // 2
    z = jnp.zeros(w3.shape[:-1] + (half,), w3.dtype)
    return jnp.concatenate(
        [w3[..., :MLA_NOPE], w3[..., MLA_NOPE:MLA_NOPE + half], z, w3[..., MLA_NOPE + half:], z], axis=-1)


def _mla_mixer(hn, cos_t, sin_t, w_in, q_lat_g, kv_lat_g, w_uq, w_ukv, q_norm_g, k_norm_g, batch, seq):
    d = hn.shape[1]
    half = MLA_ROPE // 2
    wq = w_in[:, :MLA_Q_RANK].astype(BF16)
    wkv = w_in[:, MLA_Q_RANK:MLA_Q_RANK + MLA_KV_RANK].astype(BF16)
    wpe_raw = w_in[:, MLA_Q_RANK + MLA_KV_RANK:]
    z = jnp.zeros((d, half), w_in.dtype)
    wpe = jnp.concatenate([wpe_raw[:, :half], z, wpe_raw[:, half:], z], axis=1).astype(BF16)
    qn, kvn, pe = _mla_in(hn, wq, wkv, wpe, q_lat_g, kv_lat_g)
    w_uq_s = _slot_cols(w_uq.reshape(MLA_Q_RANK, MLA_HEADS, MLA_QK)).reshape(
        MLA_Q_RANK, MLA_HEADS * MLA_SLOT).astype(BF16)
    w_ukv3 = w_ukv.reshape(MLA_KV_RANK, MLA_HEADS, MLA_NOPE + MLA_V)
    wk = w_ukv3[:, :, :MLA_NOPE].reshape(MLA_KV_RANK, MLA_HEADS * MLA_NOPE).astype(BF16)
    wv = w_ukv3[:, :, MLA_NOPE:].reshape(MLA_KV_RANK, MLA_HEADS * MLA_V).astype(BF16)
    gq = _slot_cols(q_norm_g).reshape(1, MLA_SLOT)
    gk = _slot_cols(k_norm_g).reshape(1, MLA_SLOT)
    q = _mla_q(qn, w_uq_s, gq, cos_t, sin_t)
    k, v = _mla_kv(kvn, wk, wv, pe, gk, cos_t, sin_t)
    return _attention(q, k, v, batch, seq)


def _gdn_conv_kernel(h_ref, w_ref, cw_ref, o_ref, halo_ref, *, tm, width, n_norm, n_q):
    @pl.when(pl.program_id(2) == 0)
    def _():
        halo_ref[...] = jnp.zeros_like(halo_ref)

    j = pl.program_id(0)
    z = _mm(h_ref[...], w_ref[...])
    y = _silu(_causal_conv(z, cw_ref, halo_ref[...], width))
    halo_ref[...] = z[tm - SUBLANES:, :]
    qscale = jnp.where(j < n_q, GDN_HEAD ** -0.5, 1.0)
    outs = []
    for g in range(y.shape[1] // GDN_HEAD):
        yg = y[:, g * GDN_HEAD:(g + 1) * GDN_HEAD]
        yn = (yg * lax.rsqrt(jnp.sum(yg * yg, axis=-1, keepdims=True) + EPS)) * qscale
        outs.append(jnp.where(j < n_norm, yn, yg))
    o_ref[...] = jnp.concatenate(outs, axis=1).astype(o_ref.dtype)


def _gdn_conv_in(hn, w_in, conv_w, batch, seq):
    t, d = hn.shape
    width = conv_w.shape[0]
    tm, tn = 1024, 512
    ns, nj = seq // tm, GDN_QKV // tn
    return pl.pallas_call(
        functools.partial(_gdn_conv_kernel, tm=tm, width=width, n_norm=2 * GDN_KD // tn, n_q=GDN_KD // tn),
        grid=(nj, batch, ns),
        in_specs=[
            pl.BlockSpec((tm, d), lambda j, b, s: (b * ns + s, 0)),
            pl.BlockSpec((d, tn), lambda j, b, s: (0, j)),
            pl.BlockSpec((width, tn), lambda j, b, s: (0, j)),
        ],
        out_specs=pl.BlockSpec((tm, tn), lambda j, b, s: (b * ns + s, j)),
        out_shape=jax.ShapeDtypeStruct((t, GDN_QKV), BF16),
        scratch_shapes=[pltpu.VMEM((SUBLANES, tn), F32)],
        compiler_params=_params(("arbitrary", "arbitrary", "arbitrary")),
        name="gdn_conv_in",
    )(hn, w_in, conv_w)


def _gdn_gates_kernel(ba_ref, alog_ref, dt_ref, beta_ref, gc_ref):
    ba = ba_ref[...]
    beta_ref[...] = _sigmoid(ba)
    g = -jnp.exp(alog_ref[...]) * _softplus(ba + dt_ref[...])
    rin = lax.broadcasted_iota(jnp.int32, g.shape, 0) % CHUNK
    dist = 1
    while dist < CHUNK:
        g = g + jnp.where(rin >= dist, pltpu.roll(g, dist, 0), 0.0)
        dist *= 2
    gc_ref[...] = g


def _gdn_gates(ba, a_log, dt_bias):
    t = ba.shape[0]
    pad = jnp.zeros((GDN_V_HEADS,), F32)
    tail = jnp.zeros((LANES - 2 * GDN_V_HEADS,), F32)
    alog_row = jnp.concatenate([pad, a_log.astype(F32), tail]).reshape(1, LANES)
    dt_row = jnp.concatenate([pad, dt_bias.astype(F32), tail]).reshape(1, LANES)
    tm = 512
    tile = pl.BlockSpec((tm, LANES), lambda i: (i, 0))
    rowspec = pl.BlockSpec((1, LANES), lambda i: (0, 0))
    return pl.pallas_call(
        _gdn_gates_kernel,
        grid=(t // tm,),
        in_specs=[tile, rowspec, rowspec],
        out_specs=[tile, tile],
        out_shape=[jax.ShapeDtypeStruct((t, LANES), F32)] * 2,
        compiler_params=_params(("arbitrary",)),
        name="gdn_gates",
    )(ba, alog_row, dt_row)


def _pair_mask(ri, ci, s):
    sh = s.bit_length() - 1
    return (((ri >> (sh + 1)) == (ci >> (sh + 1))) & (((ri >> sh) & 1) == 1) & (((ci >> sh) & 1) == 0))


def _lane_col(x, idx):
    lane = lax.broadcasted_iota(jnp.int32, x.shape, 1)
    return jnp.sum(jnp.where(lane == idx, x, 0.0), axis=1, keepdims=True)


def _gdn_chains(chains, masks):
    half, eye2, tril2, pms, pms_bd, bd_same = masks
    hd = GDN_HEAD
    n = len(chains)
    decay2, bdec = [], []
    for (_, _, _, _, b0, b1, g0, g1) in chains:
        gcat = jnp.where(half, g1, g0)
        grow = jnp.sum(jnp.where(eye2, gcat, 0.0), axis=0, keepdims=True)
        d2 = jnp.where(tril2, jnp.exp(jnp.where(tril2, gcat - grow, 0.0)), 0.0)
        decay2.append(d2)
        bdec.append(jnp.where(half, b1, b0) * d2)
    s2 = [_mm_nt(jnp.concatenate([c[0], c[1]], axis=0), jnp.concatenate([c[0], c[0]], axis=0)) for c in chains]
    m2 = [s2[i][:CHUNK] * bdec[i] for i in range(n)]
    a2 = [s2[i][CHUNK:] * decay2[i] for i in range(n)]
    mst = [jnp.concatenate([m, m], axis=0) for m in m2]
    x = [jnp.where(pms[0], -m, 0.0) for m in m2]
    for lvl in range(1, len(pms)):
        bd_m = [jnp.where(pms_bd[lvl], mst[i], 0.0).astype(BF16) for i in range(n)]
        y = [jnp.where(pms[lvl], m2[i], 0.0) + _mm(x[i].astype(BF16), bd_m[i]) for i in range(n)]
        bd_x = [jnp.where(bd_same, jnp.concatenate([x[i], x[i]], axis=0), 0.0).astype(BF16) for i in range(n)]
        x = [x[i] - (y[i] + _mm(y[i].astype(BF16), bd_x[i])) for i in range(n)]
    z16 = jnp.zeros((CHUNK, hd), BF16)
    pre = []
    for (k16, q16, v0, v1, b0, b1, g0, g1) in chains:
        k = k16.astype(F32)
        eg0, eg1 = jnp.exp(g0), jnp.exp(g1)
        pre.append((v0 * b0, v1 * b1, k * (b0 * eg0), k * (b1 * eg1), eg0, eg1))
    xr = []
    for i in range(n):
        vb0, vb1, kbg0, kbg1, _, _ = pre[i]
        rhs = jnp.concatenate([
            jnp.concatenate([vb0.astype(BF16), kbg0.astype(BF16), z16, z16], axis=1),
            jnp.concatenate([z16, z16, vb1.astype(BF16), kbg1.astype(BF16)], axis=1)], axis=0)
        xr.append(_mm(x[i].astype(BF16), rhs))
    outs = []
    for i in range(n):
        k16, q16, _, _, _, _, g0, g1 = chains[i]
        vb0, vb1, kbg0, kbg1, eg0, eg1 = pre[i]
        k, q = k16.astype(F32), q16.astype(F32)
        u = [vb0 + xr[i][:, :hd], vb1 + xr[i][:, 2 * hd:3 * hd]]
        w = [kbg0 + xr[i][:, hd:2 * hd], kbg1 + xr[i][:, 3 * hd:]]
        gl0, gl1 = g0[CHUNK - 1:CHUNK, :], g1[CHUNK - 1:CHUNK, :]
        qg = [q * eg0, q * eg1]
        kg = [k * jnp.exp(gl0 - g0), k * jnp.exp(gl1 - g1)]
        outs.append((u, w, qg, kg, a2[i], [jnp.exp(gl0), jnp.exp(gl1)]))
    return outs


def _gdn_core_kernel(q_ref, k_ref, v_ref, z_ref, beta_ref, gc_ref, gn_ref, o_ref,
                     u_s, w_s, qg_s, kg_s, a_s, eg_s, o_s, *, seq, kh, group):
    n_chunks = seq // CHUNK
    hd = GDN_HEAD
    nh = 2 * kh
    pg = pl.program_id(1)

    def prep(gi, carry):
        ri = lax.broadcasted_iota(jnp.int32, (CHUNK, LANES), 0)
        li = lax.broadcasted_iota(jnp.int32, (CHUNK, LANES), 1)
        ci = li & (CHUNK - 1)
        rb = lax.broadcasted_iota(jnp.int32, (2 * CHUNK, LANES), 0)
        lb = lax.broadcasted_iota(jnp.int32, (2 * CHUNK, LANES), 1)
        bd_same = (rb >= CHUNK) == (lb >= CHUNK)
        sizes = [1, 2, 4, 8, 16, 32]
        pms = [_pair_mask(ri, ci, s) for s in sizes]
        pms_bd = [_pair_mask(rb & (CHUNK - 1), lb & (CHUNK - 1), s) & bd_same for s in sizes]
        masks = (li >= CHUNK, ri == ci, ci <= ri, pms, pms_bd, bd_same)

        base = pl.multiple_of(gi * (group * CHUNK), group * CHUNK)
        beta_g = beta_ref[pl.ds(base, group * CHUNK), :]
        gc_g = gc_ref[pl.ds(base, group * CHUNK), :]
        bcols = [_lane_col(beta_g, pg * nh + h) for h in range(nh)]
        gcols = [_lane_col(gc_g, GDN_V_HEADS + pg * nh + h) for h in range(nh)]
        chains = []
        for j in range(group):
            rows = pl.ds(base + j * CHUNK, CHUNK)
            sl = slice(j * CHUNK, (j + 1) * CHUNK)
            for kk in range(kh):
                h0, h1 = 2 * kk, 2 * kk + 1
                chains.append((k_ref[rows, kk * hd:(kk + 1) * hd], q_ref[rows, kk * hd:(kk + 1) * hd],
                               v_ref[rows, h0 * hd:(h0 + 1) * hd].astype(F32),
                               v_ref[rows, h1 * hd:(h1 + 1) * hd].astype(F32),
                               bcols[h0][sl], bcols[h1][sl], gcols[h0][sl], gcols[h1][sl]))
        res = _gdn_chains(chains, masks)
        us, ws, qgs, kgs, as_, egs = [], [], [], [], [], []
        for j in range(group):
            per = res[j * kh:(j + 1) * kh]
            us.append(jnp.concatenate([t for r in per for t in r[0]], axis=1))
            ws.append(jnp.concatenate([t for r in per for t in r[1]], axis=1).astype(BF16))
            qgs.append(jnp.concatenate([t for r in per for t in r[2]], axis=1).astype(BF16))
            kgs.append(jnp.concatenate([t for r in per for t in r[3]], axis=1).astype(BF16))
            as_.append(jnp.concatenate([r[4] for r in per], axis=1).astype(BF16))
            egs.append(jnp.concatenate([jnp.broadcast_to(e, (SUBLANES, hd)) for r in per for e in r[5]], axis=1))
        grows = pl.ds(base, group * CHUNK)
        u_s[grows, :] = jnp.concatenate(us, axis=0)
        w_s[grows, :] = jnp.concatenate(ws, axis=0)
        qg_s[grows, :] = jnp.concatenate(qgs, axis=0)
        kg_s[grows, :] = jnp.concatenate(kgs, axis=0)
        a_s[grows, :] = jnp.concatenate(as_, axis=0)
        eg_s[pl.ds(pl.multiple_of(gi * (group * SUBLANES), group * SUBLANES), group * SUBLANES), :] = (
            jnp.concatenate(egs, axis=0))
        return carry

    lax.fori_loop(0, n_chunks // group, prep, 0)

    def step(c, states):
        rows = pl.ds(pl.multiple_of(c * CHUNK, CHUNK), CHUNK)
        u, w, qg, kg, a = u_s[rows, :], w_s[rows, :], qg_s[rows, :], kg_s[rows, :], a_s[rows, :]
        dec = eg_s[pl.ds(pl.multiple_of(c * SUBLANES, SUBLANES), 1), :]
        z16 = jnp.zeros((CHUNK, hd), BF16)
        col = [slice(h * hd, (h + 1) * hd) for h in range(nh)]
        r = [_mm(jnp.concatenate([w[:, col[h]], qg[:, col[h]]], axis=0), states[h].astype(BF16))
             for h in range(nh)]
        vn = [(u[:, col[h]] - r[h][:CHUNK]).astype(BF16) for h in range(nh)]
        new_states = tuple(states[h] * dec[:, col[h]] + _mm_tn(kg[:, col[h]], vn[h]) for h in range(nh))
        outs = []
        for kk in range(kh):
            h0, h1 = 2 * kk, 2 * kk + 1
            rhs = jnp.concatenate([jnp.concatenate([vn[h0], z16], axis=1),
                                   jnp.concatenate([z16, vn[h1]], axis=1)], axis=0)
            av = _mm(a[:, kk * LANES:(kk + 1) * LANES], rhs)
            outs += [r[h0][CHUNK:] + av[:, :hd], r[h1][CHUNK:] + av[:, hd:]]
        o_s[rows, :] = jnp.concatenate(outs, axis=1)
        return new_states

    init = tuple(jnp.zeros((hd, hd), F32) for _ in range(nh))
    lax.fori_loop(0, n_chunks, step, init)

    for h in range(nh):
        cols = slice(h * hd, (h + 1) * hd)
        o = o_s[:, cols]
        on = (o * lax.rsqrt(jnp.mean(o * o, axis=-1, keepdims=True) + EPS)) * gn_ref[...]
        o_ref[:, cols] = (on * _silu(z_ref[:, cols].astype(F32))).astype(o_ref.dtype)


def _gdn_core(qkv, z, beta, gc, o_norm_g, batch, seq):
    t = qkv.shape[0]
    kh, group = 2, 8
    nh = 2 * kh
    qw, vw = kh * GDN_HEAD, nh * GDN_HEAD
    koff = GDN_KD // qw
    voff = 2 * GDN_KD // vw
    n_chunks = seq // CHUNK
    return pl.pallas_call(
        functools.partial(_gdn_core_kernel, seq=seq, kh=kh, group=group),
        grid=(batch, GDN_K_HEADS // kh),
        in_specs=[
            pl.BlockSpec((seq, qw), lambda b, p: (b, p)),
            pl.BlockSpec((seq, qw), lambda b, p: (b, p + koff)),
            pl.BlockSpec((seq, vw), lambda b, p: (b, p + voff)),
            pl.BlockSpec((seq, vw), lambda b, p: (b, p)),
            pl.BlockSpec((seq, LANES), lambda b, p: (b, 0)),
            pl.BlockSpec((seq, LANES), lambda b, p: (b, 0)),
            pl.BlockSpec((1, GDN_HEAD), lambda b, p: (0, 0)),
        ],
        out_specs=pl.BlockSpec((seq, vw), lambda b, p: (b, p)),
        out_shape=jax.ShapeDtypeStruct((t, GDN_VD), BF16),
        scratch_shapes=[pltpu.VMEM((seq, vw), F32), pltpu.VMEM((seq, vw), BF16), pltpu.VMEM((seq, vw), BF16),
                        pltpu.VMEM((seq, vw), BF16), pltpu.VMEM((seq, kh * LANES), BF16),
                        pltpu.VMEM((n_chunks * SUBLANES, vw), F32), pltpu.VMEM((seq, vw), F32)],
        compiler_params=_params(("arbitrary", "arbitrary")),
        name="gdn_core",
    )(qkv, qkv, qkv, z, beta, gc, o_norm_g.reshape(1, GDN_HEAD))


def _gdn_mixer(hn, w_in, conv_w, a_log, dt_bias, o_norm_g, batch, seq):
    w16 = w_in.astype(BF16)
    qkv = _gdn_conv_in(hn, w16, conv_w, batch, seq)
    z = _matmul(hn, w16, GDN_VD, GDN_QKV, 512, BF16)
    w_ba = w_in[:, GDN_QKV + GDN_VD:]
    w_ba = jnp.concatenate([w_ba, jnp.zeros((w_ba.shape[0], LANES - w_ba.shape[1]), w_ba.dtype)], axis=1)
    ba = _matmul(hn, w_ba.astype(BF16), LANES, 0, LANES, F32)
    beta, gc = _gdn_gates(ba, a_log, dt_bias)
    return _gdn_core(qkv, z, beta, gc, o_norm_g, batch, seq)


def kernel(x, c, positions, cond_w, cond_b, mod_w, mod_b, norm_g, ffn_w_in, ffn_w_out, sc_w_in, sc_conv_w, sc_w_out, mla_w_in, mla_q_lat_g, mla_kv_lat_g, mla_w_uq, mla_w_ukv, mla_q_norm_g, mla_k_norm_g, mla_w_o, lru_w_in, lru_conv_w, lru_conv_b, lru_w_a, lru_b_a, lru_w_x, lru_b_x, lru_lam, lru_w_out, gdn_w_in, gdn_conv_w, gdn_a_log, gdn_dt_bias, gdn_o_norm_g, gdn_w_out):
    batch, seq, d = x.shape
    depth = mod_w.shape[0]
    n_mixers = 4
    x2 = x.reshape(batch * seq, d)
    mod = _modulation(c, cond_w, cond_b, mod_w, mod_b).reshape(depth, batch, 3, 3, d)
    cos_t = sin_t = None

    def ffn(x2, i, sub, which):
        hn = _prenorm(x2, norm_g[i, sub], mod[i, :, sub, 1], mod[i, :, sub, 0], seq)
        hid = _swiglu_in(hn, ffn_w_in[i, which].astype(BF16))
        alpha = 0.5 * (1.0 + mod[i, :, sub, 2])
        return _matmul_residual(hid, ffn_w_out[i, which].astype(BF16), x2, alpha, seq)

    for i in range(depth):
        m, j = i % n_mixers, i // n_mixers
        x2 = ffn(x2, i, 0, 0)
        hn = _prenorm(x2, norm_g[i, 1], mod[i, :, 1, 1], mod[i, :, 1, 0], seq)
        if m == 0:
            y = _sconv_in(hn, sc_w_in[j].astype(BF16), sc_conv_w[j], batch, seq)
            w_out = sc_w_out[j]
        elif m == 1:
            if cos_t is None:
                cos_t, sin_t = _rope_tables(positions)
            y = _mla_mixer(hn, cos_t, sin_t, mla_w_in[j], mla_q_lat_g[j], mla_kv_lat_g[j], mla_w_uq[j],
                           mla_w_ukv[j], mla_q_norm_g[j], mla_k_norm_g[j], batch, seq)
            w_out = mla_w_o[j]
        elif m == 2:
            y = _lru_in(hn, lru_w_in[j].astype(BF16), lru_conv_w[j], lru_conv_b[j], lru_w_a[j].astype(BF16),
                        lru_b_a[j], lru_w_x[j].astype(BF16), lru_b_x[j], lru_lam[j], batch, seq)
            w_out = lru_w_out[j]
        else:
            y = _gdn_mixer(hn, gdn_w_in[j], gdn_conv_w[j], gdn_a_log[j], gdn_dt_bias[j], gdn_o_norm_g[j],
                           batch, seq)
            w_out = gdn_w_out[j]
        x2 = _matmul_residual(y, w_out.astype(BF16), x2, 1.0 + mod[i, :, 1, 2], seq)
        x2 = ffn(x2, i, 2, 1)
    return x2.reshape(batch, seq, d)
```

```python
import functools

import jax
import jax.numpy as jnp
import numpy as np
from jax import lax
from jax.experimental import pallas as pl
from jax.experimental.pallas import tpu as pltpu

F32 = jnp.float32
BF16 = jnp.bfloat16

EPS = 1e-6
CHUNK = 64
MLA_HEADS = 32
MLA_Q_RANK = 1024
MLA_KV_RANK = 512
MLA_NOPE = 128
MLA_ROPE = 64
MLA_V = 128
MLA_QK = MLA_NOPE + MLA_ROPE
MLA_SLOT = 256
ROPE_BASE = 10000.0
LRU_BLOCK = 256
LRU_C = 8.0
GDN_K_HEADS = 16
GDN_V_HEADS = 32
GDN_HEAD = 128
GDN_KD = GDN_K_HEADS * GDN_HEAD
GDN_VD = GDN_V_HEADS * GDN_HEAD
GDN_QKV = 2 * GDN_KD + GDN_VD

LANES = 128
SUBLANES = 8
VMEM_LIMIT = 56 * 1024 * 1024


def _params(sem):
    return pltpu.CompilerParams(dimension_semantics=sem, vmem_limit_bytes=VMEM_LIMIT)


def _mm(a, b):
    return jnp.dot(a, b, preferred_element_type=F32)


def _mm_nt(a, b):
    return lax.dot_general(a, b, (((1,), (1,)), ((), ())), preferred_element_type=F32)


def _mm_tn(a, b):
    return lax.dot_general(a, b, (((0,), (0,)), ((), ())), preferred_element_type=F32)


def _sigmoid(x):
    return 1.0 / (1.0 + jnp.exp(-x))


def _silu(x):
    return x * _sigmoid(x)


def _softplus(x):
    return jnp.maximum(x, 0.0) + jnp.log(1.0 + jnp.exp(-jnp.abs(x)))


def _gelu_tanh(x):
    c = np.sqrt(2.0 / np.pi).astype(np.float32)
    return 0.5 * x * (1.0 + jnp.tanh(c * (x + 0.044715 * (x * x * x))))


def _mod_kernel(c_ref, cw_ref, cb_ref, mw_ref, mb_ref, o_ref):
    cond = _silu(_mm(c_ref[...].astype(BF16), cw_ref[...].astype(BF16)) + cb_ref[...])
    o_ref[...] = _mm(cond.astype(BF16), mw_ref[...].astype(BF16)) + mb_ref[...]


def _modulation(c, cond_w, cond_b, mod_w, mod_b):
    depth, d_cond, n = mod_w.shape
    b, d = c.shape
    tn = d
    return pl.pallas_call(
        _mod_kernel,
        grid=(depth, n // tn),
        in_specs=[
            pl.BlockSpec((b, d), lambda i, j: (0, 0)),
            pl.BlockSpec((d, d_cond), lambda i, j: (0, 0)),
            pl.BlockSpec((1, d_cond), lambda i, j: (0, 0)),
            pl.BlockSpec((None, d_cond, tn), lambda i, j: (i, 0, j)),
            pl.BlockSpec((None, 1, tn), lambda i, j: (i, 0, j)),
        ],
        out_specs=pl.BlockSpec((None, b, tn), lambda i, j: (i, 0, j)),
        out_shape=jax.ShapeDtypeStruct((depth, b, n), F32),
        compiler_params=_params(("arbitrary", "arbitrary")),
        name="modulation",
    )(c, cond_w, cond_b.reshape(1, d_cond), mod_w, mod_b.reshape(depth, 1, n))


def _prenorm_kernel(x_ref, g_ref, sc_ref, sh_ref, o_ref):
    x = x_ref[...]
    y = x * lax.rsqrt(jnp.mean(x * x, axis=-1, keepdims=True) + EPS)
    o_ref[...] = ((y * g_ref[...]) * (1.0 + sc_ref[...]) + sh_ref[...]).astype(o_ref.dtype)


def _prenorm(x2, g, scale, shift, seq):
    t, d = x2.shape
    tm = 256
    tpb = seq // tm
    return pl.pallas_call(
        _prenorm_kernel,
        grid=(t // tm,),
        in_specs=[
            pl.BlockSpec((tm, d), lambda i: (i, 0)),
            pl.BlockSpec((1, d), lambda i: (0, 0)),
            pl.BlockSpec((None, 1, d), lambda i: (i // tpb, 0, 0)),
            pl.BlockSpec((None, 1, d), lambda i: (i // tpb, 0, 0)),
        ],
        out_specs=pl.BlockSpec((tm, d), lambda i: (i, 0)),
        out_shape=jax.ShapeDtypeStruct((t, d), BF16),
        compiler_params=_params(("arbitrary",)),
        name="prenorm",
    )(x2, g.reshape(1, d), scale[:, None, :], shift[:, None, :])


def _swiglu_kernel(x_ref, g_ref, sc_ref, sh_ref, wg_ref, wu_ref, o_ref, hn_ref, *, rows):
    @pl.when(pl.program_id(1) == 0)
    def _():
        gmod = g_ref[...] * (1.0 + sc_ref[...])

        def norm_rows(r, carry):
            blk = pl.ds(pl.multiple_of(r * rows, rows), rows)
            x = x_ref[blk, :].astype(F32)
            y = x * lax.rsqrt(jnp.mean(x * x, axis=-1, keepdims=True) + EPS)
            hn_ref[blk, :] = (y * gmod + sh_ref[...]).astype(hn_ref.dtype)
            return carry

        lax.fori_loop(0, x_ref.shape[0] // rows, norm_rows, 0, unroll=2)

    a = hn_ref[...]
    gt = _mm(a, wg_ref[...])
    up = _mm(a, wu_ref[...])
    o_ref[...] = (_silu(gt) * up).astype(o_ref.dtype)


def _swiglu_in(x16, g, scale, shift, seq, w_in, wsel=()):
    t, d = x16.shape
    f = w_in.shape[-1] // 2
    tm, tn = 1024, 512
    nj = f // tn
    tpb = seq // tm
    lead = (None,) * len(wsel)
    return pl.pallas_call(
        functools.partial(_swiglu_kernel, rows=64),
        grid=(t // tm, nj),
        in_specs=[
            pl.BlockSpec((tm, d), lambda i, j: (i, 0)),
            pl.BlockSpec((1, d), lambda i, j: (0, 0)),
            pl.BlockSpec((None, 1, d), lambda i, j: (i // tpb, 0, 0)),
            pl.BlockSpec((None, 1, d), lambda i, j: (i // tpb, 0, 0)),
            pl.BlockSpec(lead + (d, tn), lambda i, j: wsel + (0, j)),
            pl.BlockSpec(lead + (d, tn), lambda i, j: wsel + (0, j + nj)),
        ],
        out_specs=pl.BlockSpec((tm, tn), lambda i, j: (i, j)),
        out_shape=jax.ShapeDtypeStruct((t, f), BF16),
        scratch_shapes=[pltpu.VMEM((tm, d), BF16)],
        compiler_params=_params(("arbitrary", "arbitrary")),
        name="swiglu_in",
    )(x16, g.reshape(1, d), scale[:, None, :], shift[:, None, :], w_in, w_in)


def _mm_res_kernel(a_ref, w_ref, r_ref, al_ref, o_ref, o16_ref):
    out = r_ref[...] + al_ref[...] * _mm(a_ref[...], w_ref[...])
    o_ref[...] = out
    o16_ref[...] = out.astype(o16_ref.dtype)


def _matmul_residual(a, w, res, alpha, seq, wsel=()):
    t, k = a.shape
    n = w.shape[-1]
    tm, tn = 1024, 512
    tpb = seq // tm
    lead = (None,) * len(wsel)
    tile = pl.BlockSpec((tm, tn), lambda i, j: (i, j))
    return pl.pallas_call(
        _mm_res_kernel,
        grid=(t // tm, n // tn),
        in_specs=[
            pl.BlockSpec((tm, k), lambda i, j: (i, 0)),
            pl.BlockSpec(lead + (k, tn), lambda i, j: wsel + (0, j)),
            tile,
            pl.BlockSpec((None, 1, tn), lambda i, j: (i // tpb, 0, j)),
        ],
        out_specs=[tile, tile],
        out_shape=[jax.ShapeDtypeStruct((t, n), F32), jax.ShapeDtypeStruct((t, n), BF16)],
        compiler_params=_params(("arbitrary", "arbitrary")),
        name="matmul_residual",
    )(a, w, res, alpha[:, None, :])


def _mm_kernel(a_ref, w_ref, o_ref):
    o_ref[...] = _mm(a_ref[...], w_ref[...]).astype(o_ref.dtype)


def _matmul(a, w, n, col_off, tn, out_dtype):
    t, k = a.shape
    tm = 1024
    joff = col_off // tn
    return pl.pallas_call(
        _mm_kernel,
        grid=(t // tm, n // tn),
        in_specs=[
            pl.BlockSpec((tm, k), lambda i, j: (i, 0)),
            pl.BlockSpec((k, tn), lambda i, j: (0, j + joff)),
        ],
        out_specs=pl.BlockSpec((tm, tn), lambda i, j: (i, j)),
        out_shape=jax.ShapeDtypeStruct((t, n), out_dtype),
        compiler_params=_params(("arbitrary", "arbitrary")),
        name="matmul",
    )(a, w)


def _shift_rows(z, halo, k, row8):
    zk = pltpu.roll(z, k, 0)
    hk = pltpu.roll(halo, k, 0)
    top = jnp.where(row8 < k, hk, zk[:SUBLANES])
    return jnp.concatenate([top, zk[SUBLANES:]], axis=0)


def _causal_conv(z, cw_ref, halo, width):
    row8 = lax.broadcasted_iota(jnp.int32, (SUBLANES, z.shape[1]), 0)
    acc = z * cw_ref[width - 1:width, :]
    for k in range(1, width):
        acc = acc + _shift_rows(z, halo, k, row8) * cw_ref[width - 1 - k:width - k, :]
    return acc


def _sconv_kernel(h_ref, wb_ref, wc_ref, wx_ref, cw_ref, o_ref, halo_ref, *, tm, width):
    @pl.when(pl.program_id(2) == 0)
    def _():
        halo_ref[...] = jnp.zeros_like(halo_ref)

    a = h_ref[...]
    bg = _mm(a, wb_ref[...])
    z = _mm(a, wc_ref[...]) * _mm(a, wx_ref[...])
    conv = _causal_conv(z, cw_ref, halo_ref[...], width)
    halo_ref[...] = z[tm - SUBLANES:, :]
    o_ref[...] = (bg * conv).astype(o_ref.dtype)


def _sconv_in(hn, w_in, conv_w, batch, seq):
    t, d = hn.shape
    width = conv_w.shape[0]
    tm, tn = 1024, 256
    ns, nj = seq // tm, d // tn
    return pl.pallas_call(
        functools.partial(_sconv_kernel, tm=tm, width=width),
        grid=(nj, batch, ns),
        in_specs=[
            pl.BlockSpec((tm, d), lambda j, b, s: (b * ns + s, 0)),
            pl.BlockSpec((d, tn), lambda j, b, s: (0, j)),
            pl.BlockSpec((d, tn), lambda j, b, s: (0, j + nj)),
            pl.BlockSpec((d, tn), lambda j, b, s: (0, j + 2 * nj)),
            pl.BlockSpec((width, tn), lambda j, b, s: (0, j)),
        ],
        out_specs=pl.BlockSpec((tm, tn), lambda j, b, s: (b * ns + s, j)),
        out_shape=jax.ShapeDtypeStruct((t, d), BF16),
        scratch_shapes=[pltpu.VMEM((SUBLANES, tn), F32)],
        compiler_params=_params(("arbitrary", "arbitrary", "arbitrary")),
        name="sconv_in",
    )(hn, w_in, w_in, w_in, conv_w)


def _scan_rows(av, bv):
    n = av.shape[0]
    row = lax.broadcasted_iota(jnp.int32, av.shape, 0)
    dist = 1
    while dist < n:
        if dist < SUBLANES:
            keep = row >= dist
            a_sh = jnp.where(keep, pltpu.roll(av, dist, 0), 1.0)
            b_sh = jnp.where(keep, pltpu.roll(bv, dist, 0), 0.0)
            bv = av * b_sh + bv
            av = av * a_sh
        else:
            bv = jnp.concatenate([bv[:dist], av[dist:] * bv[:n - dist] + bv[dist:]], axis=0)
            av = jnp.concatenate([av[:dist], av[dist:] * av[:n - dist]], axis=0)
        dist *= 2
    return av, bv


def _lru_kernel(h_ref, wg_ref, wx_ref, cw_ref, cb_ref, wa_ref, ba_ref, wi_ref, bi_ref, lam_ref,
                o_ref, halo_ref, hc_ref, *, tm, sub, width):
    @pl.when(pl.program_id(2) == 0)
    def _():
        halo_ref[...] = jnp.zeros_like(halo_ref)
        hc_ref[...] = jnp.zeros_like(hc_ref)

    def project(c):
        a = h_ref[c * sub:(c + 1) * sub, :]
        return _mm(a, wg_ref[...]), _mm(a, wx_ref[...])

    nsub = tm // sub
    sp = _softplus(-lam_ref[...])
    halo = halo_ref[...]
    carry = hc_ref[0:1, :]
    nxt = project(0)
    for c in range(nsub):
        gate, xr = nxt
        if c + 1 < nsub:
            nxt = project(c + 1)
        xc = _causal_conv(xr, cw_ref, halo, width) + cb_ref[...]
        halo = xr[sub - SUBLANES:, :]
        xcb = xc.astype(BF16)
        r = _sigmoid(_mm(xcb, wa_ref[...]) + ba_ref[...])
        ig = _sigmoid(_mm(xcb, wi_ref[...]) + bi_ref[...])
        log_a = (-LRU_C) * r * sp
        av, bv = _scan_rows(jnp.exp(log_a), jnp.sqrt(1.0 - jnp.exp(2.0 * log_a)) * (ig * xc))
        hs = av * carry + bv
        carry = hs[sub - 1:sub, :]
        o_ref[c * sub:(c + 1) * sub, :] = (hs * _gelu_tanh(gate)).astype(o_ref.dtype)
    halo_ref[...] = halo
    hc_ref[0:1, :] = carry


def _lru_in(hn, w_in, conv_w, conv_b, w_a, b_a, w_x, b_x, lam, batch, seq):
    t, d = hn.shape
    width = conv_w.shape[0]
    tm, tn = min(2048, seq), LRU_BLOCK
    ns, nj = seq // tm, d // tn
    row = lambda j, b, s: (0, j)
    return pl.pallas_call(
        functools.partial(_lru_kernel, tm=tm, sub=256, width=width),
        grid=(nj, batch, ns),
        in_specs=[
            pl.BlockSpec((tm, d), lambda j, b, s: (b * ns + s, 0)),
            pl.BlockSpec((d, tn), lambda j, b, s: (0, j)),
            pl.BlockSpec((d, tn), lambda j, b, s: (0, j + nj)),
            pl.BlockSpec((width, tn), row),
            pl.BlockSpec((1, tn), row),
            pl.BlockSpec((None, tn, tn), lambda j, b, s: (j, 0, 0)),
            pl.BlockSpec((1, tn), row),
            pl.BlockSpec((None, tn, tn), lambda j, b, s: (j, 0, 0)),
            pl.BlockSpec((1, tn), row),
            pl.BlockSpec((1, tn), row),
        ],
        out_specs=pl.BlockSpec((tm, tn), lambda j, b, s: (b * ns + s, j)),
        out_shape=jax.ShapeDtypeStruct((t, d), BF16),
        scratch_shapes=[pltpu.VMEM((SUBLANES, tn), F32), pltpu.VMEM((SUBLANES, tn), F32)],
        compiler_params=_params(("arbitrary", "arbitrary", "arbitrary")),
        name="lru_in",
    )(hn, w_in, w_in, conv_w, conv_b.reshape(1, d), w_a, b_a.reshape(1, d), w_x, b_x.reshape(1, d),
      lam.reshape(1, d))


def _rope_slot(x, cos_t, sin_t):
    return x * cos_t + pltpu.roll(x, LANES // 2, 1) * sin_t


def _rope_kernel(p_ref, inv_ref, msk_ref, sgn_ref, c_ref, s_ref):
    ang = p_ref[...] * inv_ref[...]
    c_ref[...] = jnp.cos(ang) * msk_ref[...]
    s_ref[...] = jnp.sin(ang) * sgn_ref[...]


def _rope_tables(positions):
    b, s = positions.shape
    t = b * s
    half = MLA_ROPE // 2
    inv = ROPE_BASE ** (-jnp.arange(0, MLA_ROPE, 2, dtype=F32) / MLA_ROPE)
    z = jnp.zeros((half,), F32)
    o = jnp.ones((half,), F32)
    inv_row = jnp.concatenate([inv, z, inv, z]).reshape(1, LANES)
    msk_row = jnp.concatenate([o, z, o, z]).reshape(1, LANES)
    sgn_row = jnp.concatenate([-o, z, o, z]).reshape(1, LANES)
    pos = jnp.broadcast_to(positions.astype(F32).reshape(t, 1), (t, LANES))
    tm = 1024
    rowspec = pl.BlockSpec((1, LANES), lambda i: (0, 0))
    tile = pl.BlockSpec((tm, LANES), lambda i: (i, 0))
    return pl.pallas_call(
        _rope_kernel,
        grid=(t // tm,),
        in_specs=[tile, rowspec, rowspec, rowspec],
        out_specs=[tile, tile],
        out_shape=[jax.ShapeDtypeStruct((t, LANES), F32)] * 2,
        compiler_params=_params(("arbitrary",)),
        name="rope_tables",
    )(pos, inv_row, msk_row, sgn_row)


def _rms_rows(x, g):
    return (x * lax.rsqrt(jnp.mean(x * x, axis=-1, keepdims=True) + EPS)) * g


def _mla_in_kernel(h_ref, wq_ref, wkv_ref, wpe_ref, gq_ref, gkv_ref, q_ref, kv_ref, pe_ref):
    a = h_ref[...]
    q_ref[...] = _rms_rows(_mm(a, wq_ref[...]), gq_ref[...]).astype(q_ref.dtype)
    kv_ref[...] = _rms_rows(_mm(a, wkv_ref[...]), gkv_ref[...]).astype(kv_ref.dtype)
    pe_ref[...] = _mm(a, wpe_ref[...])


def _mla_in(hn, wq, wkv, wpe, gq, gkv):
    t, d = hn.shape
    tm = 512
    full = lambda i: (0, 0)
    return pl.pallas_call(
        _mla_in_kernel,
        grid=(t // tm,),
        in_specs=[
            pl.BlockSpec((tm, d), lambda i: (i, 0)),
            pl.BlockSpec((d, MLA_Q_RANK), full),
            pl.BlockSpec((d, MLA_KV_RANK), full),
            pl.BlockSpec((d, LANES), full),
            pl.BlockSpec((1, MLA_Q_RANK), full),
            pl.BlockSpec((1, MLA_KV_RANK), full),
        ],
        out_specs=[
            pl.BlockSpec((tm, MLA_Q_RANK), lambda i: (i, 0)),
            pl.BlockSpec((tm, MLA_KV_RANK), lambda i: (i, 0)),
            pl.BlockSpec((tm, LANES), lambda i: (i, 0)),
        ],
        out_shape=[
            jax.ShapeDtypeStruct((t, MLA_Q_RANK), BF16),
            jax.ShapeDtypeStruct((t, MLA_KV_RANK), BF16),
            jax.ShapeDtypeStruct((t, LANES), F32),
        ],
        compiler_params=_params(("arbitrary",)),
        name="mla_in",
    )(hn, wq, wkv, wpe, gq.reshape(1, -1), gkv.reshape(1, -1))


def _mla_q_kernel(a_ref, w_ref, g_ref, c_ref, s_ref, o_ref, *, heads, sub):
    g = g_ref[...]
    for c in range(a_ref.shape[0] // sub):
        rows = slice(c * sub, (c + 1) * sub)
        acc = _mm(a_ref[rows, :], w_ref[...])
        cos_t, sin_t = c_ref[rows, :], s_ref[rows, :]
        outs = []
        for h in range(heads):
            x = acc[:, h * MLA_SLOT:(h + 1) * MLA_SLOT]
            r = lax.rsqrt(jnp.sum(x * x, axis=-1, keepdims=True) * (1.0 / MLA_QK) + EPS)
            xn = (x * r) * g
            outs.append(xn[:, :MLA_NOPE])
            outs.append(_rope_slot(xn[:, MLA_NOPE:], cos_t, sin_t))
        o_ref[rows, :] = jnp.concatenate(outs, axis=1).astype(o_ref.dtype)


def _mla_q(qn, w_uq, g_slot, cos_t, sin_t):
    t, k = qn.shape
    n = w_uq.shape[1]
    tm, heads = 1024, 2
    tn = heads * MLA_SLOT
    return pl.pallas_call(
        functools.partial(_mla_q_kernel, heads=heads, sub=256),
        grid=(t // tm, n // tn),
        in_specs=[
            pl.BlockSpec((tm, k), lambda i, j: (i, 0)),
            pl.BlockSpec((k, tn), lambda i, j: (0, j)),
            pl.BlockSpec((1, MLA_SLOT), lambda i, j: (0, 0)),
            pl.BlockSpec((tm, LANES), lambda i, j: (i, 0)),
            pl.BlockSpec((tm, LANES), lambda i, j: (i, 0)),
        ],
        out_specs=pl.BlockSpec((tm, tn), lambda i, j: (i, j)),
        out_shape=jax.ShapeDtypeStruct((t, n), BF16),
        compiler_params=_params(("arbitrary", "arbitrary")),
        name="mla_q",
    )(qn, w_uq, g_slot, cos_t, sin_t)


def _mla_kv_kernel(a_ref, wk_ref, wv_ref, pe_ref, g_ref, c_ref, s_ref, k_ref, v_ref, *, heads, sub):
    g = g_ref[...]
    for c in range(a_ref.shape[0] // sub):
        rows = slice(c * sub, (c + 1) * sub)
        a = a_ref[rows, :]
        kacc = _mm(a, wk_ref[...])
        v_ref[rows, :] = _mm(a, wv_ref[...]).astype(v_ref.dtype)
        pe = pe_ref[rows, :]
        pe_ss = jnp.sum(pe * pe, axis=-1, keepdims=True)
        cos_t, sin_t = c_ref[rows, :], s_ref[rows, :]
        outs = []
        for h in range(heads):
            x = kacc[:, h * MLA_NOPE:(h + 1) * MLA_NOPE]
            r = lax.rsqrt((jnp.sum(x * x, axis=-1, keepdims=True) + pe_ss) * (1.0 / MLA_QK) + EPS)
            outs.append((x * r) * g[:, :MLA_NOPE])
            outs.append(_rope_slot((pe * r) * g[:, MLA_NOPE:], cos_t, sin_t))
        k_ref[rows, :] = jnp.concatenate(outs, axis=1).astype(k_ref.dtype)


def _mla_kv(kvn, wk, wv, pe, g_slot, cos_t, sin_t):
    t, k = kvn.shape
    tm, heads = 1024, 2
    nj = MLA_HEADS // heads
    return pl.pallas_call(
        functools.partial(_mla_kv_kernel, heads=heads, sub=256),
        grid=(t // tm, nj),
        in_specs=[
            pl.BlockSpec((tm, k), lambda i, j: (i, 0)),
            pl.BlockSpec((k, heads * MLA_NOPE), lambda i, j: (0, j)),
            pl.BlockSpec((k, heads * MLA_V), lambda i, j: (0, j)),
            pl.BlockSpec((tm, LANES), lambda i, j: (i, 0)),
            pl.BlockSpec((1, MLA_SLOT), lambda i, j: (0, 0)),
            pl.BlockSpec((tm, LANES), lambda i, j: (i, 0)),
            pl.BlockSpec((tm, LANES), lambda i, j: (i, 0)),
        ],
        out_specs=[
            pl.BlockSpec((tm, heads * MLA_SLOT), lambda i, j: (i, j)),
            pl.BlockSpec((tm, heads * MLA_V), lambda i, j: (i, j)),
        ],
        out_shape=[
            jax.ShapeDtypeStruct((t, MLA_HEADS * MLA_SLOT), BF16),
            jax.ShapeDtypeStruct((t, MLA_HEADS * MLA_V), BF16),
        ],
        compiler_params=_params(("arbitrary", "arbitrary")),
        name="mla_kv",
    )(kvn, wk, wv, pe, g_slot, cos_t, sin_t)


def _attn_kernel(q_ref, k_ref, v_ref, o_ref, *, seq, tq, scale):
    rch = lax.broadcasted_iota(jnp.int32, (tq, tq), 0) // CHUNK
    cch = lax.broadcasted_iota(jnp.int32, (tq, tq), 1) // CHUNK
    visible = cch <= rch
    nq = seq // tq

    def scores(qi):
        lo, hi = qi * tq, (qi + 1) * tq
        q = q_ref[lo:hi, :]
        return _mm_nt(q, k_ref[lo:hi, :]), (_mm_nt(q, k_ref[0:lo, :]) if qi > 0 else None)

    nxt = scores(0)
    for qi in range(nq):
        lo, hi = qi * tq, (qi + 1) * tq
        sd_raw, so_raw = nxt
        if qi + 1 < nq:
            nxt = scores(qi + 1)
        sd = jnp.where(visible, sd_raw * scale, -1e30)
        m = jnp.max(sd, axis=-1, keepdims=True)
        if qi > 0:
            so = so_raw * scale
            m = jnp.maximum(m, jnp.max(so, axis=-1, keepdims=True))
        pd = jnp.exp(sd - m)
        den = jnp.sum(pd, axis=-1, keepdims=True)
        acc = _mm(pd.astype(BF16), v_ref[lo:hi, :])
        if qi > 0:
            po = jnp.exp(so - m)
            den = den + jnp.sum(po, axis=-1, keepdims=True)
            acc = acc + _mm(po.astype(BF16), v_ref[0:lo, :])
        o_ref[lo:hi, :] = (acc / den).astype(o_ref.dtype)


def _attention(q, k, v, batch, seq):
    t = q.shape[0]
    tq = 256
    return pl.pallas_call(
        functools.partial(_attn_kernel, seq=seq, tq=tq, scale=MLA_QK ** -0.5),
        grid=(batch, MLA_HEADS),
        in_specs=[
            pl.BlockSpec((seq, MLA_SLOT), lambda b, h: (b, h)),
            pl.BlockSpec((seq, MLA_SLOT), lambda b, h: (b, h)),
            pl.BlockSpec((seq, MLA_V), lambda b, h: (b, h)),
        ],
        out_specs=pl.BlockSpec((seq, MLA_V), lambda b, h: (b, h)),
        out_shape=jax.ShapeDtypeStruct((t, MLA_HEADS * MLA_V), BF16),
        compiler_params=_params(("arbitrary", "arbitrary")),
        name="mla_attention",
    )(q, k, v)


def _slot_cols(w3):
    half = MLA_ROPE // 2
    z = jnp.zeros(w3.shape[:-1] + (half,), w3.dtype)
    return jnp.concatenate(
        [w3[..., :MLA_NOPE], w3[..., MLA_NOPE:MLA_NOPE + half], z, w3[..., MLA_NOPE + half:], z], axis=-1)


def _mla_mixer(hn, cos_t, sin_t, w_in, q_lat_g, kv_lat_g, w_uq, w_ukv, q_norm_g, k_norm_g, batch, seq):
    d = hn.shape[1]
    half = MLA_ROPE // 2
    wq = w_in[:, :MLA_Q_RANK].astype(BF16)
    wkv = w_in[:, MLA_Q_RANK:MLA_Q_RANK + MLA_KV_RANK].astype(BF16)
    wpe_raw = w_in[:, MLA_Q_RANK + MLA_KV_RANK:]
    z = jnp.zeros((d, half), w_in.dtype)
    wpe = jnp.concatenate([wpe_raw[:, :half], z, wpe_raw[:, half:], z], axis=1).astype(BF16)
    qn, kvn, pe = _mla_in(hn, wq, wkv, wpe, q_lat_g, kv_lat_g)
    w_uq_s = _slot_cols(w_uq.reshape(MLA_Q_RANK, MLA_HEADS, MLA_QK)).reshape(
        MLA_Q_RANK, MLA_HEADS * MLA_SLOT).astype(BF16)
    w_ukv3 = w_ukv.reshape(MLA_KV_RANK, MLA_HEADS, MLA_NOPE + MLA_V)
    wk = w_ukv3[:, :, :MLA_NOPE].reshape(MLA_KV_RANK, MLA_HEADS * MLA_NOPE).astype(BF16)
    wv = w_ukv3[:, :, MLA_NOPE:].reshape(MLA_KV_RANK, MLA_HEADS * MLA_V).astype(BF16)
    gq = _slot_cols(q_norm_g).reshape(1, MLA_SLOT)
    gk = _slot_cols(k_norm_g).reshape(1, MLA_SLOT)
    q = _mla_q(qn, w_uq_s, gq, cos_t, sin_t)
    k, v = _mla_kv(kvn, wk, wv, pe, gk, cos_t, sin_t)
    return _attention(q, k, v, batch, seq)


def _gdn_conv_kernel(h_ref, w_ref, cw_ref, o_ref, halo_ref, *, tm, sub, width, n_norm, n_q):
    @pl.when(pl.program_id(2) == 0)
    def _():
        halo_ref[...] = jnp.zeros_like(halo_ref)

    j = pl.program_id(0)
    qscale = jnp.where(j < n_q, GDN_HEAD ** -0.5, 1.0)
    halo = halo_ref[...]
    for c in range(tm // sub):
        rows = slice(c * sub, (c + 1) * sub)
        z = _mm(h_ref[rows, :], w_ref[...])
        y = _silu(_causal_conv(z, cw_ref, halo, width))
        halo = z[sub - SUBLANES:, :]
        outs = []
        for g in range(y.shape[1] // GDN_HEAD):
            yg = y[:, g * GDN_HEAD:(g + 1) * GDN_HEAD]
            yn = (yg * lax.rsqrt(jnp.sum(yg * yg, axis=-1, keepdims=True) + EPS)) * qscale
            outs.append(jnp.where(j < n_norm, yn, yg))
        o_ref[rows, :] = jnp.concatenate(outs, axis=1).astype(o_ref.dtype)
    halo_ref[...] = halo


def _gdn_conv_in(hn, w_in, conv_w, batch, seq):
    t, d = hn.shape
    width = conv_w.shape[0]
    tm, tn = min(2048, seq), 512
    ns, nj = seq // tm, GDN_QKV // tn
    return pl.pallas_call(
        functools.partial(_gdn_conv_kernel, tm=tm, sub=256, width=width, n_norm=2 * GDN_KD // tn,
                          n_q=GDN_KD // tn),
        grid=(nj, batch, ns),
        in_specs=[
            pl.BlockSpec((tm, d), lambda j, b, s: (b * ns + s, 0)),
            pl.BlockSpec((d, tn), lambda j, b, s: (0, j)),
            pl.BlockSpec((width, tn), lambda j, b, s: (0, j)),
        ],
        out_specs=pl.BlockSpec((tm, tn), lambda j, b, s: (b * ns + s, j)),
        out_shape=jax.ShapeDtypeStruct((t, GDN_QKV), BF16),
        scratch_shapes=[pltpu.VMEM((SUBLANES, tn), F32)],
        compiler_params=_params(("arbitrary", "arbitrary", "arbitrary")),
        name="gdn_conv_in",
    )(hn, w_in, conv_w)


def _gdn_gates_kernel(ba_ref, alog_ref, dt_ref, beta_ref, gc_ref):
    ba = ba_ref[...]
    beta_ref[...] = _sigmoid(ba)
    g = -jnp.exp(alog_ref[...]) * _softplus(ba + dt_ref[...])
    rin = lax.broadcasted_iota(jnp.int32, g.shape, 0) % CHUNK
    dist = 1
    while dist < CHUNK:
        g = g + jnp.where(rin >= dist, pltpu.roll(g, dist, 0), 0.0)
        dist *= 2
    gc_ref[...] = g


def _gdn_gates(ba, a_log, dt_bias):
    t = ba.shape[0]
    pad = jnp.zeros((GDN_V_HEADS,), F32)
    tail = jnp.zeros((LANES - 2 * GDN_V_HEADS,), F32)
    alog_row = jnp.concatenate([pad, a_log.astype(F32), tail]).reshape(1, LANES)
    dt_row = jnp.concatenate([pad, dt_bias.astype(F32), tail]).reshape(1, LANES)
    tm = 512
    tile = pl.BlockSpec((tm, LANES), lambda i: (i, 0))
    rowspec = pl.BlockSpec((1, LANES), lambda i: (0, 0))
    return pl.pallas_call(
        _gdn_gates_kernel,
        grid=(t // tm,),
        in_specs=[tile, rowspec, rowspec],
        out_specs=[tile, tile],
        out_shape=[jax.ShapeDtypeStruct((t, LANES), F32)] * 2,
        compiler_params=_params(("arbitrary",)),
        name="gdn_gates",
    )(ba, alog_row, dt_row)


def _pair_mask(ri, ci, s):
    sh = s.bit_length() - 1
    return (((ri >> (sh + 1)) == (ci >> (sh + 1))) & (((ri >> sh) & 1) == 1) & (((ci >> sh) & 1) == 0))


def _lane_col(x, idx):
    lane = lax.broadcasted_iota(jnp.int32, x.shape, 1)
    return jnp.sum(jnp.where(lane == idx, x, 0.0), axis=1, keepdims=True)


def _gdn_chains(chains, masks):
    half, eye2, tril2, pms, pms_bd, bd_same = masks
    hd = GDN_HEAD
    n = len(chains)
    decay2, bdec = [], []
    for (_, _, _, _, b0, b1, g0, g1) in chains:
        gcat = jnp.where(half, g1, g0)
        grow = jnp.sum(jnp.where(eye2, gcat, 0.0), axis=0, keepdims=True)
        d2 = jnp.where(tril2, jnp.exp(jnp.where(tril2, gcat - grow, 0.0)), 0.0)
        decay2.append(d2)
        bdec.append(jnp.where(half, b1, b0) * d2)
    s2 = [_mm_nt(jnp.concatenate([c[0], c[1]], axis=0), jnp.concatenate([c[0], c[0]], axis=0)) for c in chains]
    m2 = [s2[i][:CHUNK] * bdec[i] for i in range(n)]
    a2 = [s2[i][CHUNK:] * decay2[i] for i in range(n)]
    mst = [jnp.concatenate([m, m], axis=0) for m in m2]
    x = [jnp.where(pms[0], -m, 0.0) for m in m2]
    for lvl in range(1, len(pms)):
        bd_m = [jnp.where(pms_bd[lvl], mst[i], 0.0).astype(BF16) for i in range(n)]
        y = [jnp.where(pms[lvl], m2[i], 0.0) + _mm(x[i].astype(BF16), bd_m[i]) for i in range(n)]
        bd_x = [jnp.where(bd_same, jnp.concatenate([x[i], x[i]], axis=0), 0.0).astype(BF16) for i in range(n)]
        x = [x[i] - (y[i] + _mm(y[i].astype(BF16), bd_x[i])) for i in range(n)]
    z16 = jnp.zeros((CHUNK, hd), BF16)
    pre = []
    for (k16, q16, v0, v1, b0, b1, g0, g1) in chains:
        k = k16.astype(F32)
        eg0, eg1 = jnp.exp(g0), jnp.exp(g1)
        pre.append((v0 * b0, v1 * b1, k * (b0 * eg0), k * (b1 * eg1), eg0, eg1))
    xr = []
    for i in range(n):
        vb0, vb1, kbg0, kbg1, _, _ = pre[i]
        rhs = jnp.concatenate([
            jnp.concatenate([vb0.astype(BF16), kbg0.astype(BF16), z16, z16], axis=1),
            jnp.concatenate([z16, z16, vb1.astype(BF16), kbg1.astype(BF16)], axis=1)], axis=0)
        xr.append(_mm(x[i].astype(BF16), rhs))
    outs = []
    for i in range(n):
        k16, q16, _, _, _, _, g0, g1 = chains[i]
        vb0, vb1, kbg0, kbg1, eg0, eg1 = pre[i]
        k, q = k16.astype(F32), q16.astype(F32)
        u = [vb0 + xr[i][:, :hd], vb1 + xr[i][:, 2 * hd:3 * hd]]
        w = [kbg0 + xr[i][:, hd:2 * hd], kbg1 + xr[i][:, 3 * hd:]]
        gl0, gl1 = g0[CHUNK - 1:CHUNK, :], g1[CHUNK - 1:CHUNK, :]
        qg = [q * eg0, q * eg1]
        kg = [k * jnp.exp(gl0 - g0), k * jnp.exp(gl1 - g1)]
        outs.append((u, w, qg, kg, a2[i], [jnp.exp(gl0), jnp.exp(gl1)]))
    return outs


def _gdn_core_kernel(q_ref, k_ref, v_ref, z_ref, beta_ref, gc_ref, gn_ref, o_ref,
                     u_s, w_s, qg_s, kg_s, a_s, eg_s, o_s, *, seq, kh, group):
    n_chunks = seq // CHUNK
    hd = GDN_HEAD
    nh = 2 * kh
    pg = pl.program_id(1)

    def prep(gi, carry):
        ri = lax.broadcasted_iota(jnp.int32, (CHUNK, LANES), 0)
        li = lax.broadcasted_iota(jnp.int32, (CHUNK, LANES), 1)
        ci = li & (CHUNK - 1)
        rb = lax.broadcasted_iota(jnp.int32, (2 * CHUNK, LANES), 0)
        lb = lax.broadcasted_iota(jnp.int32, (2 * CHUNK, LANES), 1)
        bd_same = (rb >= CHUNK) == (lb >= CHUNK)
        sizes = [1, 2, 4, 8, 16, 32]
        pms = [_pair_mask(ri, ci, s) for s in sizes]
        pms_bd = [_pair_mask(rb & (CHUNK - 1), lb & (CHUNK - 1), s) & bd_same for s in sizes]
        masks = (li >= CHUNK, ri == ci, ci <= ri, pms, pms_bd, bd_same)

        base = pl.multiple_of(gi * (group * CHUNK), group * CHUNK)
        beta_g = beta_ref[pl.ds(base, group * CHUNK), :]
        gc_g = gc_ref[pl.ds(base, group * CHUNK), :]
        bcols = [_lane_col(beta_g, pg * nh + h) for h in range(nh)]
        gcols = [_lane_col(gc_g, GDN_V_HEADS + pg * nh + h) for h in range(nh)]
        chains = []
        for j in range(group):
            rows = pl.ds(base + j * CHUNK, CHUNK)
            sl = slice(j * CHUNK, (j + 1) * CHUNK)
            for kk in range(kh):
                h0, h1 = 2 * kk, 2 * kk + 1
                chains.append((k_ref[rows, kk * hd:(kk + 1) * hd], q_ref[rows, kk * hd:(kk + 1) * hd],
                               v_ref[rows, h0 * hd:(h0 + 1) * hd].astype(F32),
                               v_ref[rows, h1 * hd:(h1 + 1) * hd].astype(F32),
                               bcols[h0][sl], bcols[h1][sl], gcols[h0][sl], gcols[h1][sl]))
        res = _gdn_chains(chains, masks)
        us, ws, qgs, kgs, as_, egs = [], [], [], [], [], []
        for j in range(group):
            per = res[j * kh:(j + 1) * kh]
            us.append(jnp.concatenate([t for r in per for t in r[0]], axis=1))
            ws.append(jnp.concatenate([t for r in per for t in r[1]], axis=1).astype(BF16))
            qgs.append(jnp.concatenate([t for r in per for t in r[2]], axis=1).astype(BF16))
            kgs.append(jnp.concatenate([t for r in per for t in r[3]], axis=1).astype(BF16))
            as_.append(jnp.concatenate([r[4] for r in per], axis=1).astype(BF16))
            egs.append(jnp.concatenate([jnp.broadcast_to(e, (SUBLANES, hd)) for r in per for e in r[5]], axis=1))
        grows = pl.ds(base, group * CHUNK)
        u_s[grows, :] = jnp.concatenate(us, axis=0)
        w_s[grows, :] = jnp.concatenate(ws, axis=0)
        qg_s[grows, :] = jnp.concatenate(qgs, axis=0)
        kg_s[grows, :] = jnp.concatenate(kgs, axis=0)
        a_s[grows, :] = jnp.concatenate(as_, axis=0)
        eg_s[pl.ds(pl.multiple_of(gi * (group * SUBLANES), group * SUBLANES), group * SUBLANES), :] = (
            jnp.concatenate(egs, axis=0))
        return carry

    lax.fori_loop(0, n_chunks // group, prep, 0)

    def step(c, states):
        rows = pl.ds(pl.multiple_of(c * CHUNK, CHUNK), CHUNK)
        u, w, qg, kg, a = u_s[rows, :], w_s[rows, :], qg_s[rows, :], kg_s[rows, :], a_s[rows, :]
        dec = eg_s[pl.ds(pl.multiple_of(c * SUBLANES, SUBLANES), 1), :]
        z16 = jnp.zeros((CHUNK, hd), BF16)
        col = [slice(h * hd, (h + 1) * hd) for h in range(nh)]
        r = [_mm(jnp.concatenate([w[:, col[h]], qg[:, col[h]]], axis=0), states[h].astype(BF16))
             for h in range(nh)]
        vn = [(u[:, col[h]] - r[h][:CHUNK]).astype(BF16) for h in range(nh)]
        new_states = tuple(states[h] * dec[:, col[h]] + _mm_tn(kg[:, col[h]], vn[h]) for h in range(nh))
        outs = []
        for kk in range(kh):
            h0, h1 = 2 * kk, 2 * kk + 1
            rhs = jnp.concatenate([jnp.concatenate([vn[h0], z16], axis=1),
                                   jnp.concatenate([z16, vn[h1]], axis=1)], axis=0)
            av = _mm(a[:, kk * LANES:(kk + 1) * LANES], rhs)
            outs += [r[h0][CHUNK:] + av[:, :hd], r[h1][CHUNK:] + av[:, hd:]]
        o_s[rows, :] = jnp.concatenate(outs, axis=1)
        return new_states

    init = tuple(jnp.zeros((hd, hd), F32) for _ in range(nh))
    lax.fori_loop(0, n_chunks, step, init)

    for h in range(nh):
        cols = slice(h * hd, (h + 1) * hd)
        o = o_s[:, cols]
        on = (o * lax.rsqrt(jnp.mean(o * o, axis=-1, keepdims=True) + EPS)) * gn_ref[...]
        o_ref[:, cols] = (on * _silu(z_ref[:, cols].astype(F32))).astype(o_ref.dtype)


def _gdn_core(qkv, z, beta, gc, o_norm_g, batch, seq):
    t = qkv.shape[0]
    kh, group = 2, 8
    nh = 2 * kh
    qw, vw = kh * GDN_HEAD, nh * GDN_HEAD
    koff = GDN_KD // qw
    voff = 2 * GDN_KD // vw
    n_chunks = seq // CHUNK
    return pl.pallas_call(
        functools.partial(_gdn_core_kernel, seq=seq, kh=kh, group=group),
        grid=(batch, GDN_K_HEADS // kh),
        in_specs=[
            pl.BlockSpec((seq, qw), lambda b, p: (b, p)),
            pl.BlockSpec((seq, qw), lambda b, p: (b, p + koff)),
            pl.BlockSpec((seq, vw), lambda b, p: (b, p + voff)),
            pl.BlockSpec((seq, vw), lambda b, p: (b, p)),
            pl.BlockSpec((seq, LANES), lambda b, p: (b, 0)),
            pl.BlockSpec((seq, LANES), lambda b, p: (b, 0)),
            pl.BlockSpec((1, GDN_HEAD), lambda b, p: (0, 0)),
        ],
        out_specs=pl.BlockSpec((seq, vw), lambda b, p: (b, p)),
        out_shape=jax.ShapeDtypeStruct((t, GDN_VD), BF16),
        scratch_shapes=[pltpu.VMEM((seq, vw), F32), pltpu.VMEM((seq, vw), BF16), pltpu.VMEM((seq, vw), BF16),
                        pltpu.VMEM((seq, vw), BF16), pltpu.VMEM((seq, kh * LANES), BF16),
                        pltpu.VMEM((n_chunks * SUBLANES, vw), F32), pltpu.VMEM((seq, vw), F32)],
        compiler_params=_params(("arbitrary", "arbitrary")),
        name="gdn_core",
    )(qkv, qkv, qkv, z, beta, gc, o_norm_g.reshape(1, GDN_HEAD))


def _gdn_mixer(hn, w_in, conv_w, a_log, dt_bias, o_norm_g, batch, seq):
    w16 = w_in.astype(BF16)
    qkv = _gdn_conv_in(hn, w16, conv_w, batch, seq)
    z = _matmul(hn, w16, GDN_VD, GDN_QKV, 512, BF16)
    w_ba = w_in[:, GDN_QKV + GDN_VD:]
    w_ba = jnp.concatenate([w_ba, jnp.zeros((w_ba.shape[0], LANES - w_ba.shape[1]), w_ba.dtype)], axis=1)
    ba = _matmul(hn, w_ba.astype(BF16), LANES, 0, LANES, F32)
    beta, gc = _gdn_gates(ba, a_log, dt_bias)
    return _gdn_core(qkv, z, beta, gc, o_norm_g, batch, seq)


def kernel(x, c, positions, cond_w, cond_b, mod_w, mod_b, norm_g, ffn_w_in, ffn_w_out, sc_w_in, sc_conv_w, sc_w_out, mla_w_in, mla_q_lat_g, mla_kv_lat_g, mla_w_uq, mla_w_ukv, mla_q_norm_g, mla_k_norm_g, mla_w_o, lru_w_in, lru_conv_w, lru_conv_b, lru_w_a, lru_b_a, lru_w_x, lru_b_x, lru_lam, lru_w_out, gdn_w_in, gdn_conv_w, gdn_a_log, gdn_dt_bias, gdn_o_norm_g, gdn_w_out):
    batch, seq, d = x.shape
    depth = mod_w.shape[0]
    n_mixers = 4
    x2 = x.reshape(batch * seq, d)
    mod = _modulation(c, cond_w, cond_b, mod_w, mod_b).reshape(depth, batch, 3, 3, d)
    cos_t = sin_t = None

    ffn_in16 = ffn_w_in.astype(BF16)
    ffn_out16 = ffn_w_out.astype(BF16)

    def ffn(x2, x16, i, sub, which):
        hid = _swiglu_in(x16, norm_g[i, sub], mod[i, :, sub, 1], mod[i, :, sub, 0], seq, ffn_in16, (i, which))
        alpha = 0.5 * (1.0 + mod[i, :, sub, 2])
        return _matmul_residual(hid, ffn_out16, x2, alpha, seq, (i, which))

    x16 = x2.astype(BF16)
    for i in range(depth):
        m, j = i % n_mixers, i // n_mixers
        x2, _ = ffn(x2, x16, i, 0, 0)
        hn = _prenorm(x2, norm_g[i, 1], mod[i, :, 1, 1], mod[i, :, 1, 0], seq)
        if m == 0:
            y = _sconv_in(hn, sc_w_in[j].astype(BF16), sc_conv_w[j], batch, seq)
            w_out = sc_w_out[j]
        elif m == 1:
            if cos_t is None:
                cos_t, sin_t = _rope_tables(positions)
            y = _mla_mixer(hn, cos_t, sin_t, mla_w_in[j], mla_q_lat_g[j], mla_kv_lat_g[j], mla_w_uq[j],
                           mla_w_ukv[j], mla_q_norm_g[j], mla_k_norm_g[j], batch, seq)
            w_out = mla_w_o[j]
        elif m == 2:
            y = _lru_in(hn, lru_w_in[j].astype(BF16), lru_conv_w[j], lru_conv_b[j], lru_w_a[j].astype(BF16),
                        lru_b_a[j], lru_w_x[j].astype(BF16), lru_b_x[j], lru_lam[j], batch, seq)
            w_out = lru_w_out[j]
        else:
            y = _gdn_mixer(hn, gdn_w_in[j], gdn_conv_w[j], gdn_a_log[j], gdn_dt_bias[j], gdn_o_norm_g[j],
                           batch, seq)
            w_out = gdn_w_out[j]
        x2, x16 = _matmul_residual(y, w_out.astype(BF16), x2, 1.0 + mod[i, :, 1, 2], seq)
        x2, x16 = ffn(x2, x16, i, 2, 1)
    return x2.reshape(batch, seq, d)
```

```python
import functools

import jax
import jax.numpy as jnp
import numpy as np
from jax import lax
from jax.experimental import pallas as pl
from jax.experimental.pallas import tpu as pltpu

F32 = jnp.float32
BF16 = jnp.bfloat16

EPS = 1e-6
CHUNK = 64
MLA_HEADS = 32
MLA_Q_RANK = 1024
MLA_KV_RANK = 512
MLA_NOPE = 128
MLA_ROPE = 64
MLA_V = 128
MLA_QK = MLA_NOPE + MLA_ROPE
MLA_SLOT = 256
ROPE_BASE = 10000.0
LRU_BLOCK = 256
LRU_C = 8.0
GDN_K_HEADS = 16
GDN_V_HEADS = 32
GDN_HEAD = 128
GDN_KD = GDN_K_HEADS * GDN_HEAD
GDN_VD = GDN_V_HEADS * GDN_HEAD
GDN_QKV = 2 * GDN_KD + GDN_VD

LANES = 128
SUBLANES = 8
VMEM_LIMIT = 56 * 1024 * 1024


def _params(sem):
    return pltpu.CompilerParams(dimension_semantics=sem, vmem_limit_bytes=VMEM_LIMIT)


def _mm(a, b):
    return jnp.dot(a, b, preferred_element_type=F32)


def _mm_nt(a, b):
    return lax.dot_general(a, b, (((1,), (1,)), ((), ())), preferred_element_type=F32)


def _mm_tn(a, b):
    return lax.dot_general(a, b, (((0,), (0,)), ((), ())), preferred_element_type=F32)


def _sigmoid(x):
    return 1.0 / (1.0 + jnp.exp(-x))


def _silu(x):
    return x * _sigmoid(x)


def _softplus(x):
    return jnp.maximum(x, 0.0) + jnp.log(1.0 + jnp.exp(-jnp.abs(x)))


def _gelu_tanh(x):
    c = np.sqrt(2.0 / np.pi).astype(np.float32)
    return 0.5 * x * (1.0 + jnp.tanh(c * (x + 0.044715 * (x * x * x))))


def _mod_kernel(c_ref, cw_ref, cb_ref, mw_ref, mb_ref, o_ref):
    cond = _silu(_mm(c_ref[...].astype(BF16), cw_ref[...].astype(BF16)) + cb_ref[...])
    o_ref[...] = _mm(cond.astype(BF16), mw_ref[...].astype(BF16)) + mb_ref[...]


def _modulation(c, cond_w, cond_b, mod_w, mod_b):
    depth, d_cond, n = mod_w.shape
    b, d = c.shape
    tn = d
    return pl.pallas_call(
        _mod_kernel,
        grid=(depth, n // tn),
        in_specs=[
            pl.BlockSpec((b, d), lambda i, j: (0, 0)),
            pl.BlockSpec((d, d_cond), lambda i, j: (0, 0)),
            pl.BlockSpec((1, d_cond), lambda i, j: (0, 0)),
            pl.BlockSpec((None, d_cond, tn), lambda i, j: (i, 0, j)),
            pl.BlockSpec((None, 1, tn), lambda i, j: (i, 0, j)),
        ],
        out_specs=pl.BlockSpec((None, b, tn), lambda i, j: (i, 0, j)),
        out_shape=jax.ShapeDtypeStruct((depth, b, n), F32),
        compiler_params=_params(("arbitrary", "arbitrary")),
        name="modulation",
    )(c, cond_w, cond_b.reshape(1, d_cond), mod_w, mod_b.reshape(depth, 1, n))


def _prenorm_kernel(x_ref, g_ref, sc_ref, sh_ref, o_ref):
    x = x_ref[...].astype(F32)
    y = x * lax.rsqrt(jnp.mean(x * x, axis=-1, keepdims=True) + EPS)
    o_ref[...] = ((y * g_ref[...]) * (1.0 + sc_ref[...]) + sh_ref[...]).astype(o_ref.dtype)


def _prenorm(x2, g, scale, shift, seq):
    t, d = x2.shape
    tm = 256
    tpb = seq // tm
    return pl.pallas_call(
        _prenorm_kernel,
        grid=(t // tm,),
        in_specs=[
            pl.BlockSpec((tm, d), lambda i: (i, 0)),
            pl.BlockSpec((1, d), lambda i: (0, 0)),
            pl.BlockSpec((None, 1, d), lambda i: (i // tpb, 0, 0)),
            pl.BlockSpec((None, 1, d), lambda i: (i // tpb, 0, 0)),
        ],
        out_specs=pl.BlockSpec((tm, d), lambda i: (i, 0)),
        out_shape=jax.ShapeDtypeStruct((t, d), BF16),
        compiler_params=_params(("arbitrary",)),
        name="prenorm",
    )(x2, g.reshape(1, d), scale[:, None, :], shift[:, None, :])


def _swiglu_kernel(x_ref, ones_ref, g_ref, sc_ref, sh_ref, wg_ref, wu_ref, o_ref, hn_ref, *, rows):
    def weights():
        return wg_ref[...].astype(BF16), wu_ref[...].astype(BF16)

    def swiglu(a, wg, wu):
        return (_silu(_mm(a, wg)) * _mm(a, wu)).astype(o_ref.dtype)

    @pl.when(pl.program_id(1) == 0)
    def _():
        d = x_ref.shape[1]
        wg, wu = weights()
        gmod = (g_ref[...] * (1.0 + sc_ref[...])).astype(BF16)
        sh = sh_ref[...].astype(BF16)
        for r in range(x_ref.shape[0] // rows):
            blk = slice(r * rows, (r + 1) * rows)
            xb = x_ref[blk, :]
            ss = _mm(xb * xb, ones_ref[...])
            rinv = lax.rsqrt(ss * (1.0 / d) + EPS).astype(BF16)
            parts = []
            for c in range(d // LANES):
                cs = slice(c * LANES, (c + 1) * LANES)
                parts.append((xb[:, cs] * rinv) * gmod[:, cs] + sh[:, cs])
            hn = jnp.concatenate(parts, axis=1)
            hn_ref[blk, :] = hn
            o_ref[blk, :] = swiglu(hn, wg, wu)

    @pl.when(pl.program_id(1) > 0)
    def _():
        o_ref[...] = swiglu(hn_ref[...], *weights())


def _swiglu_in(x16, g, scale, shift, seq, w_in, wsel=()):
    t, d = x16.shape
    f = w_in.shape[-1] // 2
    tm, tn = 1024, 256
    nj = f // tn
    tpb = seq // tm
    lead = (None,) * len(wsel)
    return pl.pallas_call(
        functools.partial(_swiglu_kernel, rows=256),
        grid=(t // tm, nj),
        in_specs=[
            pl.BlockSpec((tm, d), lambda i, j: (i, 0)),
            pl.BlockSpec((d, LANES), lambda i, j: (0, 0)),
            pl.BlockSpec((1, d), lambda i, j: (0, 0)),
            pl.BlockSpec((None, 1, d), lambda i, j: (i // tpb, 0, 0)),
            pl.BlockSpec((None, 1, d), lambda i, j: (i // tpb, 0, 0)),
            pl.BlockSpec(lead + (d, tn), lambda i, j: wsel + (0, j)),
            pl.BlockSpec(lead + (d, tn), lambda i, j: wsel + (0, j + nj)),
        ],
        out_specs=pl.BlockSpec((tm, tn), lambda i, j: (i, j)),
        out_shape=jax.ShapeDtypeStruct((t, f), BF16),
        scratch_shapes=[pltpu.VMEM((tm, d), BF16)],
        compiler_params=_params(("arbitrary", "arbitrary")),
        name="swiglu_in",
    )(x16, jnp.ones((d, LANES), BF16), g.reshape(1, d), scale[:, None, :], shift[:, None, :], w_in, w_in)


def _mm_res_kernel(a_ref, w_ref, r_ref, al_ref, o_ref, o16_ref):
    out = r_ref[...] + al_ref[...] * _mm(a_ref[...], w_ref[...])
    o_ref[...] = out
    o16_ref[...] = out.astype(o16_ref.dtype)


def _matmul_residual(a, w, res, alpha, seq, wsel=()):
    t, k = a.shape
    n = w.shape[-1]
    tm, tn = 1024, 512
    tpb = seq // tm
    lead = (None,) * len(wsel)
    tile = pl.BlockSpec((tm, tn), lambda i, j: (i, j))
    return pl.pallas_call(
        _mm_res_kernel,
        grid=(t // tm, n // tn),
        in_specs=[
            pl.BlockSpec((tm, k), lambda i, j: (i, 0)),
            pl.BlockSpec(lead + (k, tn), lambda i, j: wsel + (0, j)),
            tile,
            pl.BlockSpec((None, 1, tn), lambda i, j: (i // tpb, 0, j)),
        ],
        out_specs=[tile, tile],
        out_shape=[jax.ShapeDtypeStruct((t, n), F32), jax.ShapeDtypeStruct((t, n), BF16)],
        compiler_params=_params(("arbitrary", "arbitrary")),
        name="matmul_residual",
    )(a, w, res, alpha[:, None, :])


def _mm_kernel(a_ref, w_ref, o_ref):
    o_ref[...] = _mm(a_ref[...], w_ref[...]).astype(o_ref.dtype)


def _matmul(a, w, n, col_off, tn, out_dtype):
    t, k = a.shape
    tm = 1024
    joff = col_off // tn
    return pl.pallas_call(
        _mm_kernel,
        grid=(t // tm, n // tn),
        in_specs=[
            pl.BlockSpec((tm, k), lambda i, j: (i, 0)),
            pl.BlockSpec((k, tn), lambda i, j: (0, j + joff)),
        ],
        out_specs=pl.BlockSpec((tm, tn), lambda i, j: (i, j)),
        out_shape=jax.ShapeDtypeStruct((t, n), out_dtype),
        compiler_params=_params(("arbitrary", "arbitrary")),
        name="matmul",
    )(a, w)


def _shift_rows(z, halo, k, row8):
    zk = pltpu.roll(z, k, 0)
    hk = pltpu.roll(halo, k, 0)
    top = jnp.where(row8 < k, hk, zk[:SUBLANES])
    return jnp.concatenate([top, zk[SUBLANES:]], axis=0)


def _causal_conv(z, cw_ref, halo, width):
    row8 = lax.broadcasted_iota(jnp.int32, (SUBLANES, z.shape[1]), 0)
    acc = z * cw_ref[width - 1:width, :]
    for k in range(1, width):
        acc = acc + _shift_rows(z, halo, k, row8) * cw_ref[width - 1 - k:width - k, :]
    return acc


def _sconv_kernel(h_ref, wb_ref, wc_ref, wx_ref, cw_ref, o_ref, halo_ref, *, tm, width):
    @pl.when(pl.program_id(2) == 0)
    def _():
        halo_ref[...] = jnp.zeros_like(halo_ref)

    a = h_ref[...]
    bg = _mm(a, wb_ref[...])
    z = _mm(a, wc_ref[...]) * _mm(a, wx_ref[...])
    conv = _causal_conv(z, cw_ref, halo_ref[...], width)
    halo_ref[...] = z[tm - SUBLANES:, :]
    o_ref[...] = (bg * conv).astype(o_ref.dtype)


def _sconv_in(hn, w_in, conv_w, batch, seq):
    t, d = hn.shape
    width = conv_w.shape[0]
    tm, tn = 1024, 256
    ns, nj = seq // tm, d // tn
    return pl.pallas_call(
        functools.partial(_sconv_kernel, tm=tm, width=width),
        grid=(nj, batch, ns),
        in_specs=[
            pl.BlockSpec((tm, d), lambda j, b, s: (b * ns + s, 0)),
            pl.BlockSpec((d, tn), lambda j, b, s: (0, j)),
            pl.BlockSpec((d, tn), lambda j, b, s: (0, j + nj)),
            pl.BlockSpec((d, tn), lambda j, b, s: (0, j + 2 * nj)),
            pl.BlockSpec((width, tn), lambda j, b, s: (0, j)),
        ],
        out_specs=pl.BlockSpec((tm, tn), lambda j, b, s: (b * ns + s, j)),
        out_shape=jax.ShapeDtypeStruct((t, d), BF16),
        scratch_shapes=[pltpu.VMEM((SUBLANES, tn), F32)],
        compiler_params=_params(("arbitrary", "arbitrary", "arbitrary")),
        name="sconv_in",
    )(hn, w_in, w_in, w_in, conv_w)


def _scan_rows(av, bv):
    n = av.shape[0]
    row = lax.broadcasted_iota(jnp.int32, av.shape, 0)
    dist = 1
    while dist < n:
        if dist < SUBLANES:
            keep = row >= dist
            a_sh = jnp.where(keep, pltpu.roll(av, dist, 0), 1.0)
            b_sh = jnp.where(keep, pltpu.roll(bv, dist, 0), 0.0)
            bv = av * b_sh + bv
            av = av * a_sh
        else:
            bv = jnp.concatenate([bv[:dist], av[dist:] * bv[:n - dist] + bv[dist:]], axis=0)
            av = jnp.concatenate([av[:dist], av[dist:] * av[:n - dist]], axis=0)
        dist *= 2
    return av, bv


def _lru_kernel(h_ref, wg_ref, wx_ref, cw_ref, cb_ref, wa_ref, ba_ref, wi_ref, bi_ref, lam_ref,
                o_ref, halo_ref, hc_ref, *, tm, sub, ep, width):
    @pl.when(pl.program_id(2) == 0)
    def _():
        halo_ref[...] = jnp.zeros_like(halo_ref)
        hc_ref[...] = jnp.zeros_like(hc_ref)

    def project(c):
        a = h_ref[c * sub:(c + 1) * sub, :]
        return _mm(a, wg_ref[...]), _mm(a, wx_ref[...])

    nsub = tm // sub
    sp = _softplus(-lam_ref[...])
    halo = halo_ref[...]
    carry = hc_ref[0:1, :]
    nxt = project(0)
    for c in range(nsub):
        gate_c, xr_c = nxt
        if c + 1 < nsub:
            nxt = project(c + 1)
        for hlf in range(sub // ep):
            gate, xr = gate_c[hlf * ep:(hlf + 1) * ep], xr_c[hlf * ep:(hlf + 1) * ep]
            xc = _causal_conv(xr, cw_ref, halo, width) + cb_ref[...]
            halo = xr[ep - SUBLANES:, :]
            xcb = xc.astype(BF16)
            r = _sigmoid(_mm(xcb, wa_ref[...]) + ba_ref[...])
            ig = _sigmoid(_mm(xcb, wi_ref[...]) + bi_ref[...])
            log_a = (-LRU_C) * r * sp
            av, bv = _scan_rows(jnp.exp(log_a), jnp.sqrt(1.0 - jnp.exp(2.0 * log_a)) * (ig * xc))
            hs = av * carry + bv
            carry = hs[ep - 1:ep, :]
            lo = c * sub + hlf * ep
            o_ref[lo:lo + ep, :] = (hs * _gelu_tanh(gate)).astype(o_ref.dtype)
    halo_ref[...] = halo
    hc_ref[0:1, :] = carry


def _lru_in(hn, w_in, conv_w, conv_b, w_a, b_a, w_x, b_x, lam, batch, seq):
    t, d = hn.shape
    width = conv_w.shape[0]
    tm, tn = min(2048, seq), LRU_BLOCK
    ns, nj = seq // tm, d // tn
    row = lambda j, b, s: (0, j)
    return pl.pallas_call(
        functools.partial(_lru_kernel, tm=tm, sub=256, ep=128, width=width),
        grid=(nj, batch, ns),
        in_specs=[
            pl.BlockSpec((tm, d), lambda j, b, s: (b * ns + s, 0)),
            pl.BlockSpec((d, tn), lambda j, b, s: (0, j)),
            pl.BlockSpec((d, tn), lambda j, b, s: (0, j + nj)),
            pl.BlockSpec((width, tn), row),
            pl.BlockSpec((1, tn), row),
            pl.BlockSpec((None, tn, tn), lambda j, b, s: (j, 0, 0)),
            pl.BlockSpec((1, tn), row),
            pl.BlockSpec((None, tn, tn), lambda j, b, s: (j, 0, 0)),
            pl.BlockSpec((1, tn), row),
            pl.BlockSpec((1, tn), row),
        ],
        out_specs=pl.BlockSpec((tm, tn), lambda j, b, s: (b * ns + s, j)),
        out_shape=jax.ShapeDtypeStruct((t, d), BF16),
        scratch_shapes=[pltpu.VMEM((SUBLANES, tn), F32), pltpu.VMEM((SUBLANES, tn), F32)],
        compiler_params=_params(("arbitrary", "arbitrary", "arbitrary")),
        name="lru_in",
    )(hn, w_in, w_in, conv_w, conv_b.reshape(1, d), w_a, b_a.reshape(1, d), w_x, b_x.reshape(1, d),
      lam.reshape(1, d))


def _rope_slot(x, cos_t, sin_t):
    return x * cos_t + pltpu.roll(x, LANES // 2, 1) * sin_t


def _rope_kernel(p_ref, inv_ref, msk_ref, sgn_ref, c_ref, s_ref):
    ang = p_ref[...] * inv_ref[...]
    c_ref[...] = jnp.cos(ang) * msk_ref[...]
    s_ref[...] = jnp.sin(ang) * sgn_ref[...]


def _rope_tables(positions):
    b, s = positions.shape
    t = b * s
    half = MLA_ROPE // 2
    inv = ROPE_BASE ** (-jnp.arange(0, MLA_ROPE, 2, dtype=F32) / MLA_ROPE)
    z = jnp.zeros((half,), F32)
    o = jnp.ones((half,), F32)
    inv_row = jnp.concatenate([inv, z, inv, z]).reshape(1, LANES)
    msk_row = jnp.concatenate([o, z, o, z]).reshape(1, LANES)
    sgn_row = jnp.concatenate([-o, z, o, z]).reshape(1, LANES)
    pos = jnp.broadcast_to(positions.astype(F32).reshape(t, 1), (t, LANES))
    tm = 1024
    rowspec = pl.BlockSpec((1, LANES), lambda i: (0, 0))
    tile = pl.BlockSpec((tm, LANES), lambda i: (i, 0))
    return pl.pallas_call(
        _rope_kernel,
        grid=(t // tm,),
        in_specs=[tile, rowspec, rowspec, rowspec],
        out_specs=[tile, tile],
        out_shape=[jax.ShapeDtypeStruct((t, LANES), F32)] * 2,
        compiler_params=_params(("arbitrary",)),
        name="rope_tables",
    )(pos, inv_row, msk_row, sgn_row)


def _rms_rows(x, g):
    return (x * lax.rsqrt(jnp.mean(x * x, axis=-1, keepdims=True) + EPS)) * g


def _mla_in_kernel(h_ref, wq_ref, wkv_ref, wpe_ref, gq_ref, gkv_ref, q_ref, kv_ref, pe_ref):
    a = h_ref[...]
    q_ref[...] = _rms_rows(_mm(a, wq_ref[...]), gq_ref[...]).astype(q_ref.dtype)
    kv_ref[...] = _rms_rows(_mm(a, wkv_ref[...]), gkv_ref[...]).astype(kv_ref.dtype)
    pe_ref[...] = _mm(a, wpe_ref[...])


def _mla_in(hn, wq, wkv, wpe, gq, gkv):
    t, d = hn.shape
    tm = 512
    full = lambda i: (0, 0)
    return pl.pallas_call(
        _mla_in_kernel,
        grid=(t // tm,),
        in_specs=[
            pl.BlockSpec((tm, d), lambda i: (i, 0)),
            pl.BlockSpec((d, MLA_Q_RANK), full),
            pl.BlockSpec((d, MLA_KV_RANK), full),
            pl.BlockSpec((d, LANES), full),
            pl.BlockSpec((1, MLA_Q_RANK), full),
            pl.BlockSpec((1, MLA_KV_RANK), full),
        ],
        out_specs=[
            pl.BlockSpec((tm, MLA_Q_RANK), lambda i: (i, 0)),
            pl.BlockSpec((tm, MLA_KV_RANK), lambda i: (i, 0)),
            pl.BlockSpec((tm, LANES), lambda i: (i, 0)),
        ],
        out_shape=[
            jax.ShapeDtypeStruct((t, MLA_Q_RANK), BF16),
            jax.ShapeDtypeStruct((t, MLA_KV_RANK), BF16),
            jax.ShapeDtypeStruct((t, LANES), F32),
        ],
        compiler_params=_params(("arbitrary",)),
        name="mla_in",
    )(hn, wq, wkv, wpe, gq.reshape(1, -1), gkv.reshape(1, -1))


def _mla_q_kernel(a_ref, w_ref, g_ref, c_ref, s_ref, o_ref, *, heads, sub):
    g = g_ref[...]
    for c in range(a_ref.shape[0] // sub):
        rows = slice(c * sub, (c + 1) * sub)
        acc = _mm(a_ref[rows, :], w_ref[...])
        cos_t, sin_t = c_ref[rows, :], s_ref[rows, :]
        outs = []
        for h in range(heads):
            x = acc[:, h * MLA_SLOT:(h + 1) * MLA_SLOT]
            r = lax.rsqrt(jnp.sum(x * x, axis=-1, keepdims=True) * (1.0 / MLA_QK) + EPS)
            xn = (x * r) * g
            outs.append(xn[:, :MLA_NOPE])
            outs.append(_rope_slot(xn[:, MLA_NOPE:], cos_t, sin_t))
        o_ref[rows, :] = jnp.concatenate(outs, axis=1).astype(o_ref.dtype)


def _mla_q(qn, w_uq, g_slot, cos_t, sin_t):
    t, k = qn.shape
    n = w_uq.shape[1]
    tm, heads = 1024, 2
    tn = heads * MLA_SLOT
    return pl.pallas_call(
        functools.partial(_mla_q_kernel, heads=heads, sub=256),
        grid=(t // tm, n // tn),
        in_specs=[
            pl.BlockSpec((tm, k), lambda i, j: (i, 0)),
            pl.BlockSpec((k, tn), lambda i, j: (0, j)),
            pl.BlockSpec((1, MLA_SLOT), lambda i, j: (0, 0)),
            pl.BlockSpec((tm, LANES), lambda i, j: (i, 0)),
            pl.BlockSpec((tm, LANES), lambda i, j: (i, 0)),
        ],
        out_specs=pl.BlockSpec((tm, tn), lambda i, j: (i, j)),
        out_shape=jax.ShapeDtypeStruct((t, n), BF16),
        compiler_params=_params(("arbitrary", "arbitrary")),
        name="mla_q",
    )(qn, w_uq, g_slot, cos_t, sin_t)


def _mla_kv_kernel(a_ref, wk_ref, wv_ref, pe_ref, g_ref, c_ref, s_ref, k_ref, v_ref, *, heads, sub):
    g = g_ref[...]
    for c in range(a_ref.shape[0] // sub):
        rows = slice(c * sub, (c + 1) * sub)
        a = a_ref[rows, :]
        kacc = _mm(a, wk_ref[...])
        v_ref[rows, :] = _mm(a, wv_ref[...]).astype(v_ref.dtype)
        pe = pe_ref[rows, :]
        pe_ss = jnp.sum(pe * pe, axis=-1, keepdims=True)
        cos_t, sin_t = c_ref[rows, :], s_ref[rows, :]
        outs = []
        for h in range(heads):
            x = kacc[:, h * MLA_NOPE:(h + 1) * MLA_NOPE]
            r = lax.rsqrt((jnp.sum(x * x, axis=-1, keepdims=True) + pe_ss) * (1.0 / MLA_QK) + EPS)
            outs.append((x * r) * g[:, :MLA_NOPE])
            outs.append(_rope_slot((pe * r) * g[:, MLA_NOPE:], cos_t, sin_t))
        k_ref[rows, :] = jnp.concatenate(outs, axis=1).astype(k_ref.dtype)


def _mla_kv(kvn, wk, wv, pe, g_slot, cos_t, sin_t):
    t, k = kvn.shape
    tm, heads = 1024, 2
    nj = MLA_HEADS // heads
    return pl.pallas_call(
        functools.partial(_mla_kv_kernel, heads=heads, sub=256),
        grid=(t // tm, nj),
        in_specs=[
            pl.BlockSpec((tm, k), lambda i, j: (i, 0)),
            pl.BlockSpec((k, heads * MLA_NOPE), lambda i, j: (0, j)),
            pl.BlockSpec((k, heads * MLA_V), lambda i, j: (0, j)),
            pl.BlockSpec((tm, LANES), lambda i, j: (i, 0)),
            pl.BlockSpec((1, MLA_SLOT), lambda i, j: (0, 0)),
            pl.BlockSpec((tm, LANES), lambda i, j: (i, 0)),
            pl.BlockSpec((tm, LANES), lambda i, j: (i, 0)),
        ],
        out_specs=[
            pl.BlockSpec((tm, heads * MLA_SLOT), lambda i, j: (i, j)),
            pl.BlockSpec((tm, heads * MLA_V), lambda i, j: (i, j)),
        ],
        out_shape=[
            jax.ShapeDtypeStruct((t, MLA_HEADS * MLA_SLOT), BF16),
            jax.ShapeDtypeStruct((t, MLA_HEADS * MLA_V), BF16),
        ],
        compiler_params=_params(("arbitrary", "arbitrary")),
        name="mla_kv",
    )(kvn, wk, wv, pe, g_slot, cos_t, sin_t)


def _attn_kernel(q_ref, k_ref, v_ref, o_ref, *, seq, tq, scale):
    rch = lax.broadcasted_iota(jnp.int32, (tq, tq), 0) // CHUNK
    cch = lax.broadcasted_iota(jnp.int32, (tq, tq), 1) // CHUNK
    visible = cch <= rch
    nq = seq // tq

    def scores(qi):
        lo, hi = qi * tq, (qi + 1) * tq
        q = q_ref[lo:hi, :]
        return _mm_nt(q, k_ref[lo:hi, :]), (_mm_nt(q, k_ref[0:lo, :]) if qi > 0 else None)

    nxt = scores(0)
    for qi in range(nq):
        lo, hi = qi * tq, (qi + 1) * tq
        sd_raw, so_raw = nxt
        if qi + 1 < nq:
            nxt = scores(qi + 1)
        sd = jnp.where(visible, sd_raw * scale, -1e30)
        m = jnp.max(sd, axis=-1, keepdims=True)
        if qi > 0:
            so = so_raw * scale
            m = jnp.maximum(m, jnp.max(so, axis=-1, keepdims=True))
        pd = jnp.exp(sd - m)
        den = jnp.sum(pd, axis=-1, keepdims=True)
        acc = _mm(pd.astype(BF16), v_ref[lo:hi, :])
        if qi > 0:
            po = jnp.exp(so - m)
            den = den + jnp.sum(po, axis=-1, keepdims=True)
            acc = acc + _mm(po.astype(BF16), v_ref[0:lo, :])
        o_ref[lo:hi, :] = (acc / den).astype(o_ref.dtype)


def _attention(q, k, v, batch, seq):
    t = q.shape[0]
    tq = 256
    return pl.pallas_call(
        functools.partial(_attn_kernel, seq=seq, tq=tq, scale=MLA_QK ** -0.5),
        grid=(batch, MLA_HEADS),
        in_specs=[
            pl.BlockSpec((seq, MLA_SLOT), lambda b, h: (b, h)),
            pl.BlockSpec((seq, MLA_SLOT), lambda b, h: (b, h)),
            pl.BlockSpec((seq, MLA_V), lambda b, h: (b, h)),
        ],
        out_specs=pl.BlockSpec((seq, MLA_V), lambda b, h: (b, h)),
        out_shape=jax.ShapeDtypeStruct((t, MLA_HEADS * MLA_V), BF16),
        compiler_params=_params(("arbitrary", "arbitrary")),
        name="mla_attention",
    )(q, k, v)


def _slot_cols(w3):
    half = MLA_ROPE // 2
    z = jnp.zeros(w3.shape[:-1] + (half,), w3.dtype)
    return jnp.concatenate(
        [w3[..., :MLA_NOPE], w3[..., MLA_NOPE:MLA_NOPE + half], z, w3[..., MLA_NOPE + half:], z], axis=-1)


def _mla_mixer(hn, cos_t, sin_t, w_in, q_lat_g, kv_lat_g, w_uq, w_ukv, q_norm_g, k_norm_g, batch, seq):
    d = hn.shape[1]
    half = MLA_ROPE // 2
    wq = w_in[:, :MLA_Q_RANK].astype(BF16)
    wkv = w_in[:, MLA_Q_RANK:MLA_Q_RANK + MLA_KV_RANK].astype(BF16)
    wpe_raw = w_in[:, MLA_Q_RANK + MLA_KV_RANK:]
    z = jnp.zeros((d, half), w_in.dtype)
    wpe = jnp.concatenate([wpe_raw[:, :half], z, wpe_raw[:, half:], z], axis=1).astype(BF16)
    qn, kvn, pe = _mla_in(hn, wq, wkv, wpe, q_lat_g, kv_lat_g)
    w_uq_s = _slot_cols(w_uq.reshape(MLA_Q_RANK, MLA_HEADS, MLA_QK)).reshape(
        MLA_Q_RANK, MLA_HEADS * MLA_SLOT).astype(BF16)
    w_ukv3 = w_ukv.reshape(MLA_KV_RANK, MLA_HEADS, MLA_NOPE + MLA_V)
    wk = w_ukv3[:, :, :MLA_NOPE].reshape(MLA_KV_RANK, MLA_HEADS * MLA_NOPE).astype(BF16)
    wv = w_ukv3[:, :, MLA_NOPE:].reshape(MLA_KV_RANK, MLA_HEADS * MLA_V).astype(BF16)
    gq = _slot_cols(q_norm_g).reshape(1, MLA_SLOT)
    gk = _slot_cols(k_norm_g).reshape(1, MLA_SLOT)
    q = _mla_q(qn, w_uq_s, gq, cos_t, sin_t)
    k, v = _mla_kv(kvn, wk, wv, pe, gk, cos_t, sin_t)
    return _attention(q, k, v, batch, seq)


def _gdn_conv_kernel(h_ref, w_ref, cw_ref, o_ref, halo_ref, *, tm, sub, width, n_norm, n_q):
    @pl.when(pl.program_id(2) == 0)
    def _():
        halo_ref[...] = jnp.zeros_like(halo_ref)

    j = pl.program_id(0)
    qscale = jnp.where(j < n_q, GDN_HEAD ** -0.5, 1.0)
    halo = halo_ref[...]
    for c in range(tm // sub):
        rows = slice(c * sub, (c + 1) * sub)
        z = _mm(h_ref[rows, :], w_ref[...])
        y = _silu(_causal_conv(z, cw_ref, halo, width))
        halo = z[sub - SUBLANES:, :]
        outs = []
        for g in range(y.shape[1] // GDN_HEAD):
            yg = y[:, g * GDN_HEAD:(g + 1) * GDN_HEAD]
            yn = (yg * lax.rsqrt(jnp.sum(yg * yg, axis=-1, keepdims=True) + EPS)) * qscale
            outs.append(jnp.where(j < n_norm, yn, yg))
        o_ref[rows, :] = jnp.concatenate(outs, axis=1).astype(o_ref.dtype)
    halo_ref[...] = halo


def _gdn_conv_in(hn, w_in, conv_w, batch, seq):
    t, d = hn.shape
    width = conv_w.shape[0]
    tm, tn = min(2048, seq), 512
    ns, nj = seq // tm, GDN_QKV // tn
    return pl.pallas_call(
        functools.partial(_gdn_conv_kernel, tm=tm, sub=256, width=width, n_norm=2 * GDN_KD // tn,
                          n_q=GDN_KD // tn),
        grid=(nj, batch, ns),
        in_specs=[
            pl.BlockSpec((tm, d), lambda j, b, s: (b * ns + s, 0)),
            pl.BlockSpec((d, tn), lambda j, b, s: (0, j)),
            pl.BlockSpec((width, tn), lambda j, b, s: (0, j)),
        ],
        out_specs=pl.BlockSpec((tm, tn), lambda j, b, s: (b * ns + s, j)),
        out_shape=jax.ShapeDtypeStruct((t, GDN_QKV), BF16),
        scratch_shapes=[pltpu.VMEM((SUBLANES, tn), F32)],
        compiler_params=_params(("arbitrary", "arbitrary", "arbitrary")),
        name="gdn_conv_in",
    )(hn, w_in, conv_w)


def _gdn_gates_kernel(ba_ref, alog_ref, dt_ref, beta_ref, gc_ref):
    ba = ba_ref[...]
    beta_ref[...] = _sigmoid(ba)
    g = -jnp.exp(alog_ref[...]) * _softplus(ba + dt_ref[...])
    rin = lax.broadcasted_iota(jnp.int32, g.shape, 0) % CHUNK
    dist = 1
    while dist < CHUNK:
        g = g + jnp.where(rin >= dist, pltpu.roll(g, dist, 0), 0.0)
        dist *= 2
    gc_ref[...] = g


def _gdn_gates(ba, a_log, dt_bias):
    t = ba.shape[0]
    pad = jnp.zeros((GDN_V_HEADS,), F32)
    tail = jnp.zeros((LANES - 2 * GDN_V_HEADS,), F32)
    alog_row = jnp.concatenate([pad, a_log.astype(F32), tail]).reshape(1, LANES)
    dt_row = jnp.concatenate([pad, dt_bias.astype(F32), tail]).reshape(1, LANES)
    tm = 512
    tile = pl.BlockSpec((tm, LANES), lambda i: (i, 0))
    rowspec = pl.BlockSpec((1, LANES), lambda i: (0, 0))
    return pl.pallas_call(
        _gdn_gates_kernel,
        grid=(t // tm,),
        in_specs=[tile, rowspec, rowspec],
        out_specs=[tile, tile],
        out_shape=[jax.ShapeDtypeStruct((t, LANES), F32)] * 2,
        compiler_params=_params(("arbitrary",)),
        name="gdn_gates",
    )(ba, alog_row, dt_row)


def _pair_mask(ri, ci, s):
    sh = s.bit_length() - 1
    return (((ri >> (sh + 1)) == (ci >> (sh + 1))) & (((ri >> sh) & 1) == 1) & (((ci >> sh) & 1) == 0))


def _lane_col(x, idx):
    lane = lax.broadcasted_iota(jnp.int32, x.shape, 1)
    return jnp.sum(jnp.where(lane == idx, x, 0.0), axis=1, keepdims=True)


def _gdn_chains(chains, masks):
    half, eye2, tril2, pms, pms_bd, bd_same = masks
    hd = GDN_HEAD
    n = len(chains)
    decay2, bdec = [], []
    for (_, _, _, _, b0, b1, g0, g1) in chains:
        gcat = jnp.where(half, g1, g0)
        grow = jnp.sum(jnp.where(eye2, gcat, 0.0), axis=0, keepdims=True)
        d2 = jnp.where(tril2, jnp.exp(jnp.where(tril2, gcat - grow, 0.0)), 0.0)
        decay2.append(d2)
        bdec.append(jnp.where(half, b1, b0) * d2)
    s2 = [_mm_nt(jnp.concatenate([c[0], c[1]], axis=0), jnp.concatenate([c[0], c[0]], axis=0)) for c in chains]
    m2 = [s2[i][:CHUNK] * bdec[i] for i in range(n)]
    a2 = [s2[i][CHUNK:] * decay2[i] for i in range(n)]
    mst = [jnp.concatenate([m, m], axis=0) for m in m2]
    x = [jnp.where(pms[0], -m, 0.0) for m in m2]
    for lvl in range(1, len(pms)):
        bd_m = [jnp.where(pms_bd[lvl], mst[i], 0.0).astype(BF16) for i in range(n)]
        y = [jnp.where(pms[lvl], m2[i], 0.0) + _mm(x[i].astype(BF16), bd_m[i]) for i in range(n)]
        bd_x = [jnp.where(bd_same, jnp.concatenate([x[i], x[i]], axis=0), 0.0).astype(BF16) for i in range(n)]
        x = [x[i] - (y[i] + _mm(y[i].astype(BF16), bd_x[i])) for i in range(n)]
    z16 = jnp.zeros((CHUNK, hd), BF16)
    pre = []
    for (k16, q16, v0, v1, b0, b1, g0, g1) in chains:
        k = k16.astype(F32)
        eg0, eg1 = jnp.exp(g0), jnp.exp(g1)
        pre.append((v0 * b0, v1 * b1, k * (b0 * eg0), k * (b1 * eg1), eg0, eg1))
    xr = []
    for i in range(n):
        vb0, vb1, kbg0, kbg1, _, _ = pre[i]
        rhs = jnp.concatenate([
            jnp.concatenate([vb0.astype(BF16), kbg0.astype(BF16), z16, z16], axis=1),
            jnp.concatenate([z16, z16, vb1.astype(BF16), kbg1.astype(BF16)], axis=1)], axis=0)
        xr.append(_mm(x[i].astype(BF16), rhs))
    outs = []
    for i in range(n):
        k16, q16, _, _, _, _, g0, g1 = chains[i]
        vb0, vb1, kbg0, kbg1, eg0, eg1 = pre[i]
        k, q = k16.astype(F32), q16.astype(F32)
        u = [vb0 + xr[i][:, :hd], vb1 + xr[i][:, 2 * hd:3 * hd]]
        w = [kbg0 + xr[i][:, hd:2 * hd], kbg1 + xr[i][:, 3 * hd:]]
        gl0, gl1 = g0[CHUNK - 1:CHUNK, :], g1[CHUNK - 1:CHUNK, :]
        qg = [q * eg0, q * eg1]
        kg = [k * jnp.exp(gl0 - g0), k * jnp.exp(gl1 - g1)]
        outs.append((u, w, qg, kg, a2[i], [jnp.exp(gl0), jnp.exp(gl1)]))
    return outs


def _gdn_core_kernel(q_ref, k_ref, v_ref, z_ref, beta_ref, gc_ref, gn_ref, o_ref,
                     u_s, w_s, qg_s, kg_s, a_s, eg_s, o_s, *, seq, kh, group):
    n_chunks = seq // CHUNK
    hd = GDN_HEAD
    nh = 2 * kh
    pg = pl.program_id(1)

    def prep(gi, carry):
        ri = lax.broadcasted_iota(jnp.int32, (CHUNK, LANES), 0)
        li = lax.broadcasted_iota(jnp.int32, (CHUNK, LANES), 1)
        ci = li & (CHUNK - 1)
        rb = lax.broadcasted_iota(jnp.int32, (2 * CHUNK, LANES), 0)
        lb = lax.broadcasted_iota(jnp.int32, (2 * CHUNK, LANES), 1)
        bd_same = (rb >= CHUNK) == (lb >= CHUNK)
        sizes = [1, 2, 4, 8, 16, 32]
        pms = [_pair_mask(ri, ci, s) for s in sizes]
        pms_bd = [_pair_mask(rb & (CHUNK - 1), lb & (CHUNK - 1), s) & bd_same for s in sizes]
        masks = (li >= CHUNK, ri == ci, ci <= ri, pms, pms_bd, bd_same)

        base = pl.multiple_of(gi * (group * CHUNK), group * CHUNK)
        beta_g = beta_ref[pl.ds(base, group * CHUNK), :]
        gc_g = gc_ref[pl.ds(base, group * CHUNK), :]
        bcols = [_lane_col(beta_g, pg * nh + h) for h in range(nh)]
        gcols = [_lane_col(gc_g, GDN_V_HEADS + pg * nh + h) for h in range(nh)]
        chains = []
        for j in range(group):
            rows = pl.ds(base + j * CHUNK, CHUNK)
            sl = slice(j * CHUNK, (j + 1) * CHUNK)
            for kk in range(kh):
                h0, h1 = 2 * kk, 2 * kk + 1
                chains.append((k_ref[rows, kk * hd:(kk + 1) * hd], q_ref[rows, kk * hd:(kk + 1) * hd],
                               v_ref[rows, h0 * hd:(h0 + 1) * hd].astype(F32),
                               v_ref[rows, h1 * hd:(h1 + 1) * hd].astype(F32),
                               bcols[h0][sl], bcols[h1][sl], gcols[h0][sl], gcols[h1][sl]))
        res = _gdn_chains(chains, masks)
        us, ws, qgs, kgs, as_, egs = [], [], [], [], [], []
        for j in range(group):
            per = res[j * kh:(j + 1) * kh]
            us.append(jnp.concatenate([t for r in per for t in r[0]], axis=1))
            ws.append(jnp.concatenate([t for r in per for t in r[1]], axis=1).astype(BF16))
            qgs.append(jnp.concatenate([t for r in per for t in r[2]], axis=1).astype(BF16))
            kgs.append(jnp.concatenate([t for r in per for t in r[3]], axis=1).astype(BF16))
            as_.append(jnp.concatenate([r[4] for r in per], axis=1).astype(BF16))
            egs.append(jnp.concatenate([jnp.broadcast_to(e, (SUBLANES, hd)) for r in per for e in r[5]], axis=1))
        grows = pl.ds(base, group * CHUNK)
        u_s[grows, :] = jnp.concatenate(us, axis=0)
        w_s[grows, :] = jnp.concatenate(ws, axis=0)
        qg_s[grows, :] = jnp.concatenate(qgs, axis=0)
        kg_s[grows, :] = jnp.concatenate(kgs, axis=0)
        a_s[grows, :] = jnp.concatenate(as_, axis=0)
        eg_s[pl.ds(pl.multiple_of(gi * (group * SUBLANES), group * SUBLANES), group * SUBLANES), :] = (
            jnp.concatenate(egs, axis=0))
        return carry

    lax.fori_loop(0, n_chunks // group, prep, 0)

    def step(c, states):
        rows = pl.ds(pl.multiple_of(c * CHUNK, CHUNK), CHUNK)
        u, w, qg, kg, a = u_s[rows, :], w_s[rows, :], qg_s[rows, :], kg_s[rows, :], a_s[rows, :]
        dec = eg_s[pl.ds(pl.multiple_of(c * SUBLANES, SUBLANES), 1), :]
        z16 = jnp.zeros((CHUNK, hd), BF16)
        col = [slice(h * hd, (h + 1) * hd) for h in range(nh)]
        r = [_mm(jnp.concatenate([w[:, col[h]], qg[:, col[h]]], axis=0), states[h].astype(BF16))
             for h in range(nh)]
        vn = [(u[:, col[h]] - r[h][:CHUNK]).astype(BF16) for h in range(nh)]
        new_states = tuple(states[h] * dec[:, col[h]] + _mm_tn(kg[:, col[h]], vn[h]) for h in range(nh))
        outs = []
        for kk in range(kh):
            h0, h1 = 2 * kk, 2 * kk + 1
            rhs = jnp.concatenate([jnp.concatenate([vn[h0], z16], axis=1),
                                   jnp.concatenate([z16, vn[h1]], axis=1)], axis=0)
            av = _mm(a[:, kk * LANES:(kk + 1) * LANES], rhs)
            outs += [r[h0][CHUNK:] + av[:, :hd], r[h1][CHUNK:] + av[:, hd:]]
        o_s[rows, :] = jnp.concatenate(outs, axis=1)
        return new_states

    init = tuple(jnp.zeros((hd, hd), F32) for _ in range(nh))
    lax.fori_loop(0, n_chunks, step, init)

    for h in range(nh):
        cols = slice(h * hd, (h + 1) * hd)
        o = o_s[:, cols]
        on = (o * lax.rsqrt(jnp.mean(o * o, axis=-1, keepdims=True) + EPS)) * gn_ref[...]
        o_ref[:, cols] = (on * _silu(z_ref[:, cols].astype(F32))).astype(o_ref.dtype)


def _gdn_core(qkv, z, beta, gc, o_norm_g, batch, seq):
    t = qkv.shape[0]
    kh, group = 2, 8
    nh = 2 * kh
    qw, vw = kh * GDN_HEAD, nh * GDN_HEAD
    koff = GDN_KD // qw
    voff = 2 * GDN_KD // vw
    n_chunks = seq // CHUNK
    return pl.pallas_call(
        functools.partial(_gdn_core_kernel, seq=seq, kh=kh, group=group),
        grid=(batch, GDN_K_HEADS // kh),
        in_specs=[
            pl.BlockSpec((seq, qw), lambda b, p: (b, p)),
            pl.BlockSpec((seq, qw), lambda b, p: (b, p + koff)),
            pl.BlockSpec((seq, vw), lambda b, p: (b, p + voff)),
            pl.BlockSpec((seq, vw), lambda b, p: (b, p)),
            pl.BlockSpec((seq, LANES), lambda b, p: (b, 0)),
            pl.BlockSpec((seq, LANES), lambda b, p: (b, 0)),
            pl.BlockSpec((1, GDN_HEAD), lambda b, p: (0, 0)),
        ],
        out_specs=pl.BlockSpec((seq, vw), lambda b, p: (b, p)),
        out_shape=jax.ShapeDtypeStruct((t, GDN_VD), BF16),
        scratch_shapes=[pltpu.VMEM((seq, vw), F32), pltpu.VMEM((seq, vw), BF16), pltpu.VMEM((seq, vw), BF16),
                        pltpu.VMEM((seq, vw), BF16), pltpu.VMEM((seq, kh * LANES), BF16),
                        pltpu.VMEM((n_chunks * SUBLANES, vw), F32), pltpu.VMEM((seq, vw), F32)],
        compiler_params=_params(("arbitrary", "arbitrary")),
        name="gdn_core",
    )(qkv, qkv, qkv, z, beta, gc, o_norm_g.reshape(1, GDN_HEAD))


def _gdn_mixer(hn, w_in, conv_w, a_log, dt_bias, o_norm_g, batch, seq):
    w16 = w_in.astype(BF16)
    qkv = _gdn_conv_in(hn, w16, conv_w, batch, seq)
    z = _matmul(hn, w16, GDN_VD, GDN_QKV, 512, BF16)
    w_ba = w_in[:, GDN_QKV + GDN_VD:]
    w_ba = jnp.concatenate([w_ba, jnp.zeros((w_ba.shape[0], LANES - w_ba.shape[1]), w_ba.dtype)], axis=1)
    ba = _matmul(hn, w_ba.astype(BF16), LANES, 0, LANES, F32)
    beta, gc = _gdn_gates(ba, a_log, dt_bias)
    return _gdn_core(qkv, z, beta, gc, o_norm_g, batch, seq)


def kernel(x, c, positions, cond_w, cond_b, mod_w, mod_b, norm_g, ffn_w_in, ffn_w_out, sc_w_in, sc_conv_w, sc_w_out, mla_w_in, mla_q_lat_g, mla_kv_lat_g, mla_w_uq, mla_w_ukv, mla_q_norm_g, mla_k_norm_g, mla_w_o, lru_w_in, lru_conv_w, lru_conv_b, lru_w_a, lru_b_a, lru_w_x, lru_b_x, lru_lam, lru_w_out, gdn_w_in, gdn_conv_w, gdn_a_log, gdn_dt_bias, gdn_o_norm_g, gdn_w_out):
    batch, seq, d = x.shape
    depth = mod_w.shape[0]
    n_mixers = 4
    x2 = x.reshape(batch * seq, d)
    mod = _modulation(c, cond_w, cond_b, mod_w, mod_b).reshape(depth, batch, 3, 3, d)
    cos_t = sin_t = None

    ffn_out16 = ffn_w_out.astype(BF16)

    def ffn(x2, x16, i, sub, which):
        hid = _swiglu_in(x16, norm_g[i, sub], mod[i, :, sub, 1], mod[i, :, sub, 0], seq, ffn_w_in, (i, which))
        alpha = 0.5 * (1.0 + mod[i, :, sub, 2])
        return _matmul_residual(hid, ffn_out16, x2, alpha, seq, (i, which))

    x16 = x2.astype(BF16)
    for i in range(depth):
        m, j = i % n_mixers, i // n_mixers
        x2, x16 = ffn(x2, x16, i, 0, 0)
        hn = _prenorm(x16, norm_g[i, 1], mod[i, :, 1, 1], mod[i, :, 1, 0], seq)
        if m == 0:
            y = _sconv_in(hn, sc_w_in[j].astype(BF16), sc_conv_w[j], batch, seq)
            w_out = sc_w_out[j]
        elif m == 1:
            if cos_t is None:
                cos_t, sin_t = _rope_tables(positions)
            y = _mla_mixer(hn, cos_t, sin_t, mla_w_in[j], mla_q_lat_g[j], mla_kv_lat_g[j], mla_w_uq[j],
                           mla_w_ukv[j], mla_q_norm_g[j], mla_k_norm_g[j], batch, seq)
            w_out = mla_w_o[j]
        elif m == 2:
            y = _lru_in(hn, lru_w_in[j].astype(BF16), lru_conv_w[j], lru_conv_b[j], lru_w_a[j].astype(BF16),
                        lru_b_a[j], lru_w_x[j].astype(BF16), lru_b_x[j], lru_lam[j], batch, seq)
            w_out = lru_w_out[j]
        else:
            y = _gdn_mixer(hn, gdn_w_in[j], gdn_conv_w[j], gdn_a_log[j], gdn_dt_bias[j], gdn_o_norm_g[j],
                           batch, seq)
            w_out = gdn_w_out[j]
        x2, x16 = _matmul_residual(y, w_out.astype(BF16), x2, 1.0 + mod[i, :, 1, 2], seq)
        x2, x16 = ffn(x2, x16, i, 2, 1)
    return x2.reshape(batch, seq, d)
```
